```python
import jax, jax.numpy as jnp
from jax import lax
import numpy as np

D_MODEL = 4096
BATCH = 2
SEQ = 8192
DEPTH = 4

EPS = 1e-6
POOL_WIDTH = 1024
POOL_WINDOWS = (2, 4, 8, 16)
POOL_GROUP = POOL_WIDTH // len(POOL_WINDOWS)
GLA_HEADS = 4
GLA_DK = 128
GLA_DV = 256
GLA_QK_W = GLA_HEADS * GLA_DK
GLA_V_W = GLA_HEADS * GLA_DV
GLA_GATE_RANK = 16
GLA_GATE_NORMALIZER = 16.0
GLA_CHUNK = 64
FOX_HEADS = 8
FOX_DH = 128
FOX_W = FOX_HEADS * FOX_DH
FOX_BLOCK = 128
FOX_FORGET_BIAS_INIT = 2.0
N_BRANCH = 3
GATE_RANK = 256
FFN_HIDDEN = ((8 * D_MODEL // 3 + 255) // 256) * 256

SPLIT_SIZES = (POOL_WIDTH,
               GLA_QK_W, GLA_QK_W, GLA_V_W, GLA_GATE_RANK, GLA_V_W,
               FOX_W, FOX_W, FOX_W, FOX_HEADS,
               GATE_RANK)
IN_WIDTH = sum(SPLIT_SIZES)

kernel_name = "hybrid_pool_gla_fox_gated_block"


def rms_norm(x, g):
    xf = x.astype(jnp.float32)
    y = xf * lax.rsqrt(jnp.mean(xf * xf, axis=-1, keepdims=True) + EPS)
    return (y * g.astype(jnp.float32)).astype(x.dtype)


def pool_mixer(u, w_pool, s_pool):
    S_ = u.shape[1]
    uf = u.astype(jnp.float32)
    csum = jnp.cumsum(uf, axis=1)
    csum = jnp.concatenate([jnp.zeros_like(csum[:, :1]), csum], axis=1)
    pos = jnp.arange(S_)
    outs = []
    for gi, w in enumerate(POOL_WINDOWS):
        sl = slice(gi * POOL_GROUP, (gi + 1) * POOL_GROUP)
        c = csum[:, :, sl]
        start = jnp.maximum(pos + 1 - w, 0)
        window_sum = c[:, 1:] - c[:, start]
        count = (pos + 1 - start).astype(jnp.float32)
        p = (window_sum / count[None, :, None] - uf[:, :, sl]).astype(u.dtype)
        outs.append(p @ w_pool[gi])
    return jnp.concatenate(outs, axis=-1) * s_pool


def gla_mixer(q, k, v, a_low, r, w_alpha, b_alpha, g_gla):
    B_, S_, _ = q.shape
    f32 = jnp.float32
    nc = S_ // GLA_CHUNK
    log_a = jax.nn.log_sigmoid((a_low @ w_alpha + b_alpha).astype(f32)) / GLA_GATE_NORMALIZER

    def to_chunks(t, d):
        return t.astype(f32).reshape(B_, nc, GLA_CHUNK, GLA_HEADS, d).transpose(1, 0, 3, 2, 4)

    qc = to_chunks(q, GLA_DK) * (GLA_DK ** -0.5)
    kc = to_chunks(k, GLA_DK)
    vc = to_chunks(v, GLA_DV)
    gc = to_chunks(log_a, GLA_DK)
    causal = jnp.tril(jnp.ones((GLA_CHUNK, GLA_CHUNK), dtype=bool))

    def step(state, inp):
        q_, k_, v_, g_ = inp
        b = jnp.cumsum(g_, axis=-2)
        diff = b[..., :, None, :] - b[..., None, :, :]
        decay = jnp.exp(jnp.where(causal[:, :, None], diff, -jnp.inf))
        attn = jnp.einsum('bhid,bhijd->bhij', q_, k_[..., None, :, :] * decay)
        o_intra = jnp.einsum('bhij,bhjv->bhiv', attn, v_)
        o_inter = jnp.einsum('bhid,bhdv->bhiv', q_ * jnp.exp(b), state)
        b_last = b[..., -1, :]
        k_dec = k_ * jnp.exp(b_last[..., None, :] - b)
        state = state * jnp.exp(b_last)[..., None] + jnp.einsum('bhjd,bhjv->bhdv', k_dec, v_)
        return state, o_intra + o_inter

    state0 = jnp.zeros((B_, GLA_HEADS, GLA_DK, GLA_DV), f32)
    _, o = lax.scan(step, state0, (qc, kc, vc, gc))
    o = o.transpose(1, 0, 3, 2, 4).reshape(B_, S_, GLA_HEADS, GLA_DV)
    o = o * lax.rsqrt(jnp.mean(o * o, axis=-1, keepdims=True) + EPS) * g_gla.astype(f32)
    o = o.reshape(B_, S_, GLA_V_W) * jax.nn.silu(r.astype(f32))
    return o.astype(q.dtype)


def fox_mixer(q, k, v, f_logit, b_f):
    B_, S_, _ = q.shape
    f32 = jnp.float32
    scale = FOX_DH ** -0.5
    qh = q.reshape(B_, S_, FOX_HEADS, FOX_DH).transpose(0, 2, 1, 3)
    kh = k.reshape(B_, S_, FOX_HEADS, FOX_DH).transpose(0, 2, 1, 3)
    vh = v.reshape(B_, S_, FOX_HEADS, FOX_DH).transpose(0, 2, 1, 3)
    log_f = jax.nn.log_sigmoid((f_logit + b_f).astype(f32))
    c = jnp.cumsum(log_f, axis=1).transpose(0, 2, 1)
    nb = S_ // FOX_BLOCK
    q_blocks = qh.reshape(B_, FOX_HEADS, nb, FOX_BLOCK, FOX_DH).transpose(2, 0, 1, 3, 4)
    c_blocks = c.reshape(B_, FOX_HEADS, nb, FOX_BLOCK).transpose(2, 0, 1, 3)
    key_pos = jnp.arange(S_)

    def block(args):
        qb, cb, bi = args
        s = jnp.einsum('bhqd,bhkd->bhqk', qb, kh, preferred_element_type=f32) * scale
        s = s + cb[..., :, None] - c[:, :, None, :]
        q_pos = bi * FOX_BLOCK + jnp.arange(FOX_BLOCK)
        s = jnp.where(key_pos[None, :] <= q_pos[:, None], s, -jnp.inf)
        p = jax.nn.softmax(s, axis=-1)
        return jnp.einsum('bhqk,bhkd->bhqd', p.astype(vh.dtype), vh)

    o = lax.map(block, (q_blocks, c_blocks, jnp.arange(nb)))
    return o.transpose(1, 0, 3, 2, 4).reshape(B_, S_, FOX_W)


def setup_inputs(seed: int = 0) -> dict:
    key = jax.random.key(seed)
    ks = jax.random.split(key, 24)
    f32 = jnp.float32

    def dense(k, shape, fan_in):
        return jax.random.normal(k, shape, f32) * (fan_in ** -0.5)

    def gain(k, shape):
        return 1.0 + 0.02 * jax.random.normal(k, shape, f32)

    def bias(k, shape, scale=0.01):
        return scale * jax.random.normal(k, shape, f32)

    L = DEPTH
    return {
        "x": jax.random.normal(ks[0], (BATCH, SEQ, D_MODEL), f32),
        "g_mix": gain(ks[1], (L, D_MODEL)),
        "w_in": dense(ks[2], (L, D_MODEL, IN_WIDTH), D_MODEL),
        "w_pool": dense(ks[3], (L, len(POOL_WINDOWS), POOL_GROUP, POOL_GROUP), POOL_GROUP),
        "s_pool": gain(ks[4], (L, POOL_WIDTH)),
        "w_alpha": dense(ks[5], (L, GLA_GATE_RANK, GLA_QK_W), GLA_GATE_RANK),
        "b_alpha": bias(ks[6], (L, GLA_QK_W)),
        "g_gla": gain(ks[7], (L, GLA_DV)),
        "b_f": FOX_FORGET_BIAS_INIT + 0.1 * jax.random.normal(ks[8], (L, FOX_HEADS), f32),
        "w_gate_up": dense(ks[9], (L, GATE_RANK, N_BRANCH * D_MODEL), GATE_RANK),
        "b_gate": bias(ks[10], (L, N_BRANCH * D_MODEL)),
        "w_proj_pool": dense(ks[11], (L, POOL_WIDTH, D_MODEL), POOL_WIDTH),
        "w_proj_gla": dense(ks[12], (L, GLA_V_W, D_MODEL), GLA_V_W),
        "w_proj_fox": dense(ks[13], (L, FOX_W, D_MODEL), FOX_W),
        "w_o": dense(ks[14], (L, D_MODEL, D_MODEL), D_MODEL),
        "g_ffn": gain(ks[15], (L, D_MODEL)),
        "w_ffn_gate": dense(ks[16], (L, D_MODEL, FFN_HIDDEN), D_MODEL),
        "w_ffn_up": dense(ks[17], (L, D_MODEL, FFN_HIDDEN), D_MODEL),
        "w_ffn_down": dense(ks[18], (L, FFN_HIDDEN, D_MODEL), FFN_HIDDEN),
        "g_final": gain(ks[19], (D_MODEL,)),
    }


def reference(x, g_mix, w_in, w_pool, s_pool, w_alpha, b_alpha, g_gla, b_f,
              w_gate_up, b_gate, w_proj_pool, w_proj_gla, w_proj_fox, w_o,
              g_ffn, w_ffn_gate, w_ffn_up, w_ffn_down, g_final):
    B_, S_, _ = x.shape
    split_points = [int(p) for p in np.cumsum(SPLIT_SIZES)[:-1]]
    for l in range(DEPTH):
        h = rms_norm(x, g_mix[l])
        z = h @ w_in[l]
        (u_pool, q_gla, k_gla, v_gla, a_gla, r_gla,
         q_fox, k_fox, v_fox, f_fox, gate_low) = jnp.split(z, split_points, axis=-1)
        y_pool = pool_mixer(u_pool, w_pool[l], s_pool[l])
        y_gla = gla_mixer(q_gla, k_gla, v_gla, a_gla, r_gla,
                          w_alpha[l], b_alpha[l], g_gla[l])
        y_fox = fox_mixer(q_fox, k_fox, v_fox, f_fox, b_f[l])
        gates = jax.nn.sigmoid((gate_low @ w_gate_up[l] + b_gate[l]).astype(jnp.float32))
        gates = gates.reshape(B_, S_, N_BRANCH, D_MODEL).astype(x.dtype)
        merged = (gates[:, :, 0] * (y_pool @ w_proj_pool[l])
                  + gates[:, :, 1] * (y_gla @ w_proj_gla[l])
                  + gates[:, :, 2] * (y_fox @ w_proj_fox[l]))
        x = x + merged @ w_o[l]
        h2 = rms_norm(x, g_ffn[l])
        x = x + (jax.nn.silu(h2 @ w_ffn_gate[l]) * (h2 @ w_ffn_up[l])) @ w_ffn_down[l]
    return rms_norm(x, g_final)
```

```python
import functools

import jax
import jax.numpy as jnp
import numpy as np
from jax import lax
from jax.experimental import pallas as pl
from jax.experimental.pallas import tpu as pltpu

F32 = jnp.float32
BF16 = jnp.bfloat16
HIGHEST = lax.Precision.HIGHEST

D_MODEL = 4096
DEPTH = 4
EPS = 1e-6
POOL_WIDTH = 1024
POOL_WINDOWS = (2, 4, 8, 16)
POOL_GROUP = POOL_WIDTH // len(POOL_WINDOWS)
POOL_HALO = 16
GLA_HEADS = 4
GLA_DK = 128
GLA_DV = 256
GLA_QK_W = GLA_HEADS * GLA_DK
GLA_V_W = GLA_HEADS * GLA_DV
GLA_GATE_RANK = 16
GLA_GATE_NORMALIZER = 16.0
GLA_CHUNK = 64
GLA_SUB = 16
FOX_HEADS = 8
FOX_DH = 128
FOX_W = FOX_HEADS * FOX_DH
N_BRANCH = 3
GATE_RANK = 256
FFN_HIDDEN = ((8 * D_MODEL // 3 + 255) // 256) * 256

LANES = 128
Z_POOL, Z_GQ, Z_GK, Z_GV, Z_GR, Z_FQ, Z_FK, Z_FV = 0, 1024, 1536, 2048, 3072, 4096, 5120, 6144
Z_MAIN_W = 7168
ZS_GATE = 0
ZS_SMALL = GATE_RANK
ZS_A_LANE = 0
ZS_F_LANE = GLA_GATE_RANK
ZS_W = GATE_RANK + LANES

FFN_PAD = 11264

VMEM_LIMIT = 56 * 1024 * 1024


def _cparams(sem):
    return pltpu.CompilerParams(dimension_semantics=sem, vmem_limit_bytes=VMEM_LIMIT)


def _dot(a, b, precision=None):
    return jnp.dot(a, b, preferred_element_type=F32, precision=precision)


def _dot_nt(a, b):
    return lax.dot_general(a, b, (((1,), (1,)), ((), ())), preferred_element_type=F32)


def _dot_tn(a, b):
    return lax.dot_general(a, b, (((0,), (0,)), ((), ())), preferred_element_type=F32)


def _log_sigmoid(x):
    return jnp.minimum(x, 0.0) - jnp.log(1.0 + jnp.exp(-jnp.abs(x)))


def _sigmoid(x):
    return 1.0 / (1.0 + jnp.exp(-x))


def _rms_rows(x, g):
    return x * lax.rsqrt(jnp.mean(x * x, axis=-1, keepdims=True) + EPS) * g


def _inproj_kernel(x_ref, g_ref, w_ref, ws_ref, z_ref, zs_ref, h_ref):
    @pl.when(pl.program_id(1) == 0)
    def _():
        h = _rms_rows(x_ref[...], g_ref[...]).astype(BF16)
        h_ref[...] = h
        zs_ref[...] = _dot(h, ws_ref[...])

    z_ref[...] = _dot(h_ref[...], w_ref[...]).astype(BF16)


def _inproj(x, g, w_main, w_small, layer, tm=512, tn=1024):
    T, D = x.shape
    n_main = w_main.shape[-1]
    n_small = w_small.shape[-1]
    return pl.pallas_call(
        _inproj_kernel,
        grid=(T // tm, n_main // tn),
        in_specs=[
            pl.BlockSpec((tm, D), lambda i, j: (i, 0)),
            pl.BlockSpec((None, 1, D), lambda i, j: (layer, 0, 0)),
            pl.BlockSpec((None, D, tn), lambda i, j: (layer, 0, j)),
            pl.BlockSpec((None, D, n_small), lambda i, j: (layer, 0, 0)),
        ],
        out_specs=[
            pl.BlockSpec((tm, tn), lambda i, j: (i, j)),
            pl.BlockSpec((tm, n_small), lambda i, j: (i, 0)),
        ],
        out_shape=[
            jax.ShapeDtypeStruct((T, n_main), BF16),
            jax.ShapeDtypeStruct((T, n_small), F32),
        ],
        scratch_shapes=[pltpu.VMEM((tm, D), BF16)],
        compiler_params=_cparams(("parallel", "arbitrary")),
        name="inproj",
    )(x, g, w_main, w_small)


def _pool_kernel(cur_ref, prev_ref, w_ref, s_ref, o_ref, ext_ref, *, tm, blocks_per_seq):
    blk = pl.program_id(0) % blocks_per_seq
    cur = cur_ref[...].astype(F32)
    ext_ref[0:POOL_HALO, :] = jnp.where(blk == 0, 0.0, prev_ref[...].astype(F32))
    ext_ref[POOL_HALO:, :] = cur
    pos = blk * tm + lax.broadcasted_iota(jnp.int32, (tm, 1), 0)
    for gi, win in enumerate(POOL_WINDOWS):
        lo = gi * POOL_GROUP
        u = cur[:, lo:lo + POOL_GROUP]
        acc = u
        for s in range(1, win):
            acc = acc + ext_ref[POOL_HALO - s:POOL_HALO - s + tm, lo:lo + POOL_GROUP]
        count = jnp.minimum(pos + 1, win).astype(F32)
        p = (acc / count - u).astype(BF16)
        y = _dot(p, w_ref[gi]) * s_ref[:, lo:lo + POOL_GROUP]
        o_ref[:, lo:lo + POOL_GROUP] = y.astype(BF16)


def _pool(z, w_pool, s_pool, layer, seq, tm=512):
    T = z.shape[0]
    halo_per_tile = tm // POOL_HALO
    return pl.pallas_call(
        functools.partial(_pool_kernel, tm=tm, blocks_per_seq=seq // tm),
        grid=(T // tm,),
        in_specs=[
            pl.BlockSpec((tm, POOL_WIDTH), lambda i: (i, Z_POOL // POOL_WIDTH)),
            pl.BlockSpec((POOL_HALO, POOL_WIDTH),
                         lambda i: (jnp.maximum(i * halo_per_tile - 1, 0), Z_POOL // POOL_WIDTH)),
            pl.BlockSpec((None, len(POOL_WINDOWS), POOL_GROUP, POOL_GROUP), lambda i: (layer, 0, 0, 0)),
            pl.BlockSpec((None, 1, POOL_WIDTH), lambda i: (layer, 0, 0)),
        ],
        out_specs=pl.BlockSpec((tm, POOL_WIDTH), lambda i: (i, 0)),
        out_shape=jax.ShapeDtypeStruct((T, POOL_WIDTH), BF16),
        scratch_shapes=[pltpu.VMEM((tm + POOL_HALO, POOL_WIDTH), F32)],
        compiler_params=_cparams(("parallel",)),
        name="pool",
    )(z, z, w_pool, s_pool)


def _fcum_kernel(zs_ref, bf_ref, ccol_ref, crow_ref, carry_ref, *, tc, blocks_per_seq):
    @pl.when(pl.program_id(0) % blocks_per_seq == 0)
    def _():
        carry_ref[...] = jnp.zeros_like(carry_ref)

    lf = _log_sigmoid(zs_ref[...] + bf_ref[...])
    row = lax.broadcasted_iota(jnp.int32, (tc, tc), 0)
    col = lax.broadcasted_iota(jnp.int32, (tc, tc), 1)
    tril = (row >= col).astype(F32)
    c = _dot(tril, lf, precision=HIGHEST) + carry_ref[...]
    carry_ref[...] = c[tc - 1:tc, :]
    ccol_ref[...] = c
    crow_ref[...] = c.T


def _fcum(zs, bf_pad, layer, seq, tc=512):
    T = zs.shape[0]
    return pl.pallas_call(
        functools.partial(_fcum_kernel, tc=tc, blocks_per_seq=seq // tc),
        grid=(T // tc,),
        in_specs=[
            pl.BlockSpec((tc, LANES), lambda i: (i, ZS_SMALL // LANES)),
            pl.BlockSpec((None, 1, LANES), lambda i: (layer, 0, 0)),
        ],
        out_specs=[
            pl.BlockSpec((tc, LANES), lambda i: (i, 0)),
            pl.BlockSpec((LANES, tc), lambda i: (0, i)),
        ],
        out_shape=[
            jax.ShapeDtypeStruct((T, LANES), F32),
            jax.ShapeDtypeStruct((LANES, T), F32),
        ],
        scratch_shapes=[pltpu.VMEM((1, LANES), F32)],
        compiler_params=_cparams(("arbitrary",)),
        name="fcum",
    )(zs, bf_pad)


def _gla_kernel(q_ref, k_ref, v_ref, r_ref, a_ref, wa_ref, ba_ref, gg_ref, o_ref,
                st_ref, g_ref, *, tg):
    @pl.when(pl.program_id(2) == 0)
    def _():
        st_ref[...] = jnp.zeros_like(st_ref)

    logits = _dot(a_ref[...].astype(BF16), wa_ref[...]) + ba_ref[...]
    g_ref[...] = _log_sigmoid(logits) * (1.0 / GLA_GATE_NORMALIZER)

    C, SB = GLA_CHUNK, GLA_SUB
    nsb = C // SB
    row = lax.broadcasted_iota(jnp.int32, (C, C), 0)
    col = lax.broadcasted_iota(jnp.int32, (C, C), 1)
    tril = (row >= col).astype(F32)
    sdiff = row // SB - col // SB
    lane_sb = lax.broadcasted_iota(jnp.int32, (SB, C), 1)
    row_sb = lax.broadcasted_iota(jnp.int32, (SB, C), 0)
    gain = gg_ref[...]

    def rows4(parts):
        return jnp.concatenate([jnp.broadcast_to(p, (SB, GLA_DK)) for p in parts], axis=0)

    def chunk(c, carry):
        r0 = pl.multiple_of(c * C, C)
        q = q_ref[pl.ds(r0, C), :].astype(F32) * (GLA_DK ** -0.5)
        k = k_ref[pl.ds(r0, C), :].astype(F32)
        v = v_ref[pl.ds(r0, C), :]
        g = g_ref[pl.ds(r0, C), :]
        b = _dot(tril, g, precision=HIGHEST)
        edge = [jnp.zeros((1, GLA_DK), F32)] + [b[SB * (i + 1) - 1:SB * (i + 1), :] for i in range(nsb)]
        zero = edge[0]
        rb = rows4(edge[0:nsb])
        rn = rows4(edge[1:nsb + 1])
        qt = q * jnp.exp(b - rb)
        kt = (k * jnp.exp(rn - b)).astype(BF16)
        a_off = jnp.where(sdiff == 1, _dot_nt(qt.astype(BF16), kt), 0.0)
        for s in range(2, nsb):
            d = rows4([zero] * s + [jnp.exp(edge[i] - edge[i - s + 1]) for i in range(s, nsb)])
            a_off = a_off + jnp.where(sdiff == s, _dot_nt((qt * d).astype(BF16), kt), 0.0)
        a_rows = []
        for sb in range(nsb):
            sl = slice(SB * sb, SB * (sb + 1))
            q_s, k_s, b_s = q[sl], k[sl], b[sl]
            acc = a_off[sl]
            for j in range(SB):
                e = jnp.exp(jnp.minimum(b_s - b_s[j:j + 1], 0.0))
                colv = jnp.sum(q_s * (k_s[j:j + 1] * e), axis=-1, keepdims=True)
                acc = jnp.where((lane_sb == SB * sb + j) & (row_sb >= j), colv, acc)
            a_rows.append(acc)
        a = jnp.concatenate(a_rows, axis=0).astype(BF16)

        st = st_ref[...]
        o = _dot(a, v) + _dot_nt((q * jnp.exp(b)).astype(BF16), st.astype(BF16))
        b_last = edge[nsb]
        k_dec = (k * jnp.exp(b_last - b)).astype(BF16)
        st_ref[...] = st * jnp.exp(b_last) + _dot_tn(v, k_dec)

        o = o * lax.rsqrt(jnp.mean(o * o, axis=-1, keepdims=True) + EPS) * gain
        r = r_ref[pl.ds(r0, C), :].astype(F32)
        o_ref[pl.ds(r0, C), :] = (o * (r * _sigmoid(r))).astype(BF16)
        return carry

    lax.fori_loop(0, tg // C, chunk, 0)


def _gla(z, zs, wa_t, b_alpha, g_gla, layer, batch, seq, tg=512):
    T = z.shape[0]
    nb = seq // tg
    qb, kb = Z_GQ // GLA_DK, Z_GK // GLA_DK
    vb, rb = Z_GV // GLA_DV, Z_GR // GLA_DV
    return pl.pallas_call(
        functools.partial(_gla_kernel, tg=tg),
        grid=(batch, GLA_HEADS, nb),
        in_specs=[
            pl.BlockSpec((tg, GLA_DK), lambda b, h, n: (b * nb + n, qb + h)),
            pl.BlockSpec((tg, GLA_DK), lambda b, h, n: (b * nb + n, kb + h)),
            pl.BlockSpec((tg, GLA_DV), lambda b, h, n: (b * nb + n, vb + h)),
            pl.BlockSpec((tg, GLA_DV), lambda b, h, n: (b * nb + n, rb + h)),
            pl.BlockSpec((tg, LANES), lambda b, h, n: (b * nb + n, ZS_SMALL // LANES)),
            pl.BlockSpec((None, LANES, GLA_DK), lambda b, h, n: (layer, 0, h)),
            pl.BlockSpec((None, 1, GLA_DK), lambda b, h, n: (layer, 0, h)),
            pl.BlockSpec((None, 1, GLA_DV), lambda b, h, n: (layer, 0, 0)),
        ],
        out_specs=pl.BlockSpec((tg, GLA_DV), lambda b, h, n: (b * nb + n, h)),
        out_shape=jax.ShapeDtypeStruct((T, GLA_V_W), BF16),
        scratch_shapes=[pltpu.VMEM((GLA_DV, GLA_DK), F32), pltpu.VMEM((tg, GLA_DK), F32)],
        compiler_params=_cparams(("parallel", "parallel", "arbitrary")),
        name="gla",
    )(z, z, z, z, zs, wa_t, b_alpha, g_gla)


def _fox_kernel(qi_ref, ki_ref, q_ref, k_ref, v_ref, ccol_ref, crow_ref, o_ref,
                m_ref, l_ref, acc_ref, cq_ref, *, tq, tk):
    h = pl.program_id(1)
    t = pl.program_id(2)
    qi = qi_ref[t]
    ki = ki_ref[t]

    @pl.when(ki == 0)
    def _():
        m_ref[...] = jnp.full_like(m_ref, -jnp.inf)
        l_ref[...] = jnp.zeros_like(l_ref)
        acc_ref[...] = jnp.zeros_like(acc_ref)
        lane = lax.broadcasted_iota(jnp.int32, (tq, LANES), 1)
        cq_ref[...] = jnp.sum(jnp.where(lane == ZS_F_LANE + h, ccol_ref[...], 0.0),
                              axis=-1, keepdims=True)

    def step(masked):
        s = _dot_nt(q_ref[...], k_ref[...]) * (FOX_DH ** -0.5)
        s = s + cq_ref[...] - crow_ref[pl.ds(h, 1), :]
        if masked:
            row = lax.broadcasted_iota(jnp.int32, (tq, tk), 0)
            col = lax.broadcasted_iota(jnp.int32, (tq, tk), 1)
            s = jnp.where(col <= row, s, -jnp.inf)
        m_old = m_ref[...]
        m_new = jnp.maximum(m_old, jnp.max(s, axis=-1, keepdims=True))
        p = jnp.exp(s - m_new)
        alpha = jnp.exp(m_old - m_new)
        l_ref[...] = alpha * l_ref[...] + jnp.sum(p, axis=-1, keepdims=True)
        acc_ref[...] = alpha * acc_ref[...] + _dot(p.astype(BF16), v_ref[...])
        m_ref[...] = m_new

    @pl.when(ki < qi)
    def _():
        step(False)

    @pl.when(ki == qi)
    def _():
        step(True)
        o_ref[...] = (acc_ref[...] / l_ref[...]).astype(BF16)


def _fox(z, ccol, crow, batch, seq, tq=1024):
    T = z.shape[0]
    tk = tq
    nq = seq // tq
    pairs = [(qi, ki) for qi in range(nq) for ki in range(qi + 1)]
    qi_tab = jnp.asarray(np.array([p[0] for p in pairs], np.int32))
    ki_tab = jnp.asarray(np.array([p[1] for p in pairs], np.int32))
    qb, kb, vb = Z_FQ // FOX_DH, Z_FK // FOX_DH, Z_FV // FOX_DH
    f_row_blk = ZS_F_LANE // FOX_HEADS
    grid_spec = pltpu.PrefetchScalarGridSpec(
        num_scalar_prefetch=2,
        grid=(batch, FOX_HEADS, len(pairs)),
        in_specs=[
            pl.BlockSpec((tq, FOX_DH), lambda b, h, t, qt, kt: (b * nq + qt[t], qb + h)),
            pl.BlockSpec((tk, FOX_DH), lambda b, h, t, qt, kt: (b * nq + kt[t], kb + h)),
            pl.BlockSpec((tk, FOX_DH), lambda b, h, t, qt, kt: (b * nq + kt[t], vb + h)),
            pl.BlockSpec((tq, LANES), lambda b, h, t, qt, kt: (b * nq + qt[t], 0)),
            pl.BlockSpec((FOX_HEADS, tk), lambda b, h, t, qt, kt: (f_row_blk, b * nq + kt[t])),
        ],
        out_specs=pl.BlockSpec((tq, FOX_DH), lambda b, h, t, qt, kt: (b * nq + qt[t], h)),
        scratch_shapes=[
            pltpu.VMEM((tq, 1), F32), pltpu.VMEM((tq, 1), F32),
            pltpu.VMEM((tq, FOX_DH), F32), pltpu.VMEM((tq, 1), F32),
        ],
    )
    return pl.pallas_call(
        functools.partial(_fox_kernel, tq=tq, tk=tk),
        grid_spec=grid_spec,
        out_shape=jax.ShapeDtypeStruct((T, FOX_W), BF16),
        compiler_params=_cparams(("parallel", "parallel", "arbitrary")),
        name="fox",
    )(qi_tab, ki_tab, z, z, z, ccol, crow)


def _merge_kernel(yp_ref, yg_ref, yf_ref, gl_ref, wp_ref, wg_ref, wf_ref,
                  u0_ref, u1_ref, u2_ref, b0_ref, b1_ref, b2_ref, o_ref):
    gl = gl_ref[...].astype(BF16)
    out = _sigmoid(_dot(gl, u0_ref[...]) + b0_ref[...]) * _dot(yp_ref[...], wp_ref[...])
    out = out + _sigmoid(_dot(gl, u1_ref[...]) + b1_ref[...]) * _dot(yg_ref[...], wg_ref[...])
    out = out + _sigmoid(_dot(gl, u2_ref[...]) + b2_ref[...]) * _dot(yf_ref[...], wf_ref[...])
    o_ref[...] = out.astype(BF16)


def _merge(yp, yg, yf, zs, wpp, wpg, wpf, wgu, b_gate, layer, tm=1024, tn=512):
    T = yp.shape[0]
    nj = D_MODEL // tn
    y_spec = pl.BlockSpec((tm, POOL_WIDTH), lambda i, j: (i, 0))
    w_spec = pl.BlockSpec((None, POOL_WIDTH, tn), lambda i, j: (layer, 0, j))

    def u_spec(br):
        return pl.BlockSpec((None, GATE_RANK, tn), lambda i, j: (layer, 0, br * nj + j))

    def b_spec(br):
        return pl.BlockSpec((None, 1, tn), lambda i, j: (layer, 0, br * nj + j))

    return pl.pallas_call(
        _merge_kernel,
        grid=(T // tm, nj),
        in_specs=[y_spec, y_spec, y_spec,
                  pl.BlockSpec((tm, GATE_RANK), lambda i, j: (i, ZS_GATE // GATE_RANK)),
                  w_spec, w_spec, w_spec,
                  u_spec(0), u_spec(1), u_spec(2), b_spec(0), b_spec(1), b_spec(2)],
        out_specs=pl.BlockSpec((tm, tn), lambda i, j: (i, j)),
        out_shape=jax.ShapeDtypeStruct((T, D_MODEL), BF16),
        compiler_params=_cparams(("parallel", "arbitrary")),
        name="merge",
    )(yp, yg, yf, zs, wpp, wpg, wpf, wgu, wgu, wgu, b_gate, b_gate, b_gate)


def _mm_res_kernel(a_ref, w_ref, res_ref, o_ref):
    @pl.when(pl.program_id(2) == 0)
    def _():
        o_ref[...] = res_ref[...] + _dot(a_ref[...], w_ref[...])

    @pl.when(pl.program_id(2) != 0)
    def _():
        o_ref[...] += _dot(a_ref[...], w_ref[...])


def _mm_res(a, w, res, layer, tm, tn, tk, name):
    T, K = a.shape
    N = w.shape[-1]
    return pl.pallas_call(
        _mm_res_kernel,
        grid=(T // tm, N // tn, K // tk),
        in_specs=[
            pl.BlockSpec((tm, tk), lambda i, j, k: (i, k)),
            pl.BlockSpec((None, tk, tn), lambda i, j, k: (layer, k, j)),
            pl.BlockSpec((tm, tn), lambda i, j, k: (i, j)),
        ],
        out_specs=pl.BlockSpec((tm, tn), lambda i, j, k: (i, j)),
        out_shape=jax.ShapeDtypeStruct((T, N), F32),
        compiler_params=_cparams(("parallel", "parallel", "arbitrary")),
        name=name,
    )(a, w, res)


def _ffn_up_kernel(x_ref, g_ref, wg_ref, wu_ref, o_ref, h_ref):
    @pl.when(pl.program_id(1) == 0)
    def _():
        h_ref[...] = _rms_rows(x_ref[...], g_ref[...]).astype(BF16)

    h = h_ref[...]
    gate = _dot(h, wg_ref[...])
    up = _dot(h, wu_ref[...])
    o_ref[...] = (gate * _sigmoid(gate) * up).astype(BF16)


def _ffn_up(x, g, wg, wu, layer, tm=512, tn=512):
    T, D = x.shape
    N = wg.shape[-1]
    w_spec = pl.BlockSpec((None, D, tn), lambda i, j: (layer, 0, j))
    return pl.pallas_call(
        _ffn_up_kernel,
        grid=(T // tm, N // tn),
        in_specs=[
            pl.BlockSpec((tm, D), lambda i, j: (i, 0)),
            pl.BlockSpec((None, 1, D), lambda i, j: (layer, 0, 0)),
            w_spec, w_spec,
        ],
        out_specs=pl.BlockSpec((tm, tn), lambda i, j: (i, j)),
        out_shape=jax.ShapeDtypeStruct((T, N), BF16),
        scratch_shapes=[pltpu.VMEM((tm, D), BF16)],
        compiler_params=_cparams(("parallel", "arbitrary")),
        name="ffn_up",
    )(x, g, wg, wu)


def _norm_kernel(x_ref, g_ref, o_ref):
    o_ref[...] = _rms_rows(x_ref[...], g_ref[...])


def _final_norm(x, g, tm=512):
    T, D = x.shape
    return pl.pallas_call(
        _norm_kernel,
        grid=(T // tm,),
        in_specs=[pl.BlockSpec((tm, D), lambda i: (i, 0)), pl.BlockSpec((1, D), lambda i: (0, 0))],
        out_specs=pl.BlockSpec((tm, D), lambda i: (i, 0)),
        out_shape=jax.ShapeDtypeStruct((T, D), F32),
        compiler_params=_cparams(("parallel",)),
        name="final_norm",
    )(x, g)


def _split_offsets():
    sizes = (POOL_WIDTH, GLA_QK_W, GLA_QK_W, GLA_V_W, GLA_GATE_RANK, GLA_V_W,
             FOX_W, FOX_W, FOX_W, FOX_HEADS, GATE_RANK)
    offs = np.concatenate([[0], np.cumsum(sizes)])
    return [int(o) for o in offs]


def _prep_w_in(w_in):
    o = _split_offsets()
    a_lo, a_hi = o[4], o[5]
    f_lo, f_hi = o[9], o[10]
    gl_lo, gl_hi = o[10], o[11]
    main = jnp.concatenate([w_in[..., :a_lo], w_in[..., a_hi:f_lo]], axis=-1).astype(BF16)
    pad = jnp.zeros(w_in.shape[:-1] + (ZS_W - GATE_RANK - GLA_GATE_RANK - FOX_HEADS,), w_in.dtype)
    small = jnp.concatenate([w_in[..., gl_lo:gl_hi], w_in[..., a_lo:a_hi], w_in[..., f_lo:f_hi], pad],
                            axis=-1).astype(BF16)
    return main, small


def _pad_last(a, width):
    return jnp.pad(a, [(0, 0)] * (a.ndim - 1) + [(0, width - a.shape[-1])])


def kernel(x, g_mix, w_in, w_pool, s_pool, w_alpha, b_alpha, g_gla, b_f, w_gate_up, b_gate,
           w_proj_pool, w_proj_gla, w_proj_fox, w_o, g_ffn, w_ffn_gate, w_ffn_up, w_ffn_down, g_final):
    B, S, D = x.shape
    T = B * S
    L = g_mix.shape[0]
    xf = x.reshape(T, D)

    w_main, w_small = _prep_w_in(w_in)
    w_pool_b = w_pool.astype(BF16)
    wa = jnp.pad(w_alpha, ((0, 0), (ZS_A_LANE, LANES - ZS_A_LANE - GLA_GATE_RANK), (0, 0))).astype(BF16)
    bf_pad = jnp.pad(b_f, ((0, 0), (ZS_F_LANE, LANES - ZS_F_LANE - FOX_HEADS)))[:, None, :]
    wgu = w_gate_up.astype(BF16)
    wpp, wpg, wpf = (w.astype(BF16) for w in (w_proj_pool, w_proj_gla, w_proj_fox))
    wo = w_o.astype(BF16)
    wfg = _pad_last(w_ffn_gate, FFN_PAD).astype(BF16)
    wfu = _pad_last(w_ffn_up, FFN_PAD).astype(BF16)
    wfd = jnp.pad(w_ffn_down, ((0, 0), (0, FFN_PAD - FFN_HIDDEN), (0, 0))).astype(BF16)
    g_mix3, g_ffn3 = g_mix[:, None, :], g_ffn[:, None, :]
    s_pool3, b_alpha3, g_gla3, b_gate3 = (a[:, None, :] for a in (s_pool, b_alpha, g_gla, b_gate))

    for l in range(L):
        z, zs = _inproj(xf, g_mix3, w_main, w_small, l)
        y_pool = _pool(z, w_pool_b, s_pool3, l, S)
        ccol, crow = _fcum(zs, bf_pad, l, S)
        y_gla = _gla(z, zs, wa, b_alpha3, g_gla3, l, B, S)
        y_fox = _fox(z, ccol, crow, B, S)
        merged = _merge(y_pool, y_gla, y_fox, zs, wpp, wpg, wpf, wgu, b_gate3, l)
        xf = _mm_res(merged, wo, xf, l, tm=1024, tn=512, tk=D_MODEL, name="out_proj")
        hid = _ffn_up(xf, g_ffn3, wfg, wfu, l)
        xf = _mm_res(hid, wfd, xf, l, tm=1024, tn=1024, tk=FFN_PAD // 4, name="ffn_down")
    return _final_norm(xf, g_final[None, :]).reshape(B, S, D)
```

```python
import functools

import jax
import jax.numpy as jnp
import numpy as np
from jax import lax
from jax.experimental import pallas as pl
from jax.experimental.pallas import tpu as pltpu

F32 = jnp.float32
BF16 = jnp.bfloat16
HIGHEST = lax.Precision.HIGHEST

D_MODEL = 4096
DEPTH = 4
EPS = 1e-6
POOL_WIDTH = 1024
POOL_WINDOWS = (2, 4, 8, 16)
POOL_GROUP = POOL_WIDTH // len(POOL_WINDOWS)
POOL_HALO = 16
GLA_HEADS = 4
GLA_DK = 128
GLA_DV = 256
GLA_QK_W = GLA_HEADS * GLA_DK
GLA_V_W = GLA_HEADS * GLA_DV
GLA_GATE_RANK = 16
GLA_GATE_NORMALIZER = 16.0
GLA_CHUNK = 64
GLA_SUB = 16
FOX_HEADS = 8
FOX_DH = 128
FOX_W = FOX_HEADS * FOX_DH
N_BRANCH = 3
GATE_RANK = 256
FFN_HIDDEN = ((8 * D_MODEL // 3 + 255) // 256) * 256

LANES = 128
Z_POOL, Z_GQ, Z_GK, Z_GV, Z_GR, Z_FQ, Z_FK, Z_FV = 0, 1024, 1536, 2048, 3072, 4096, 5120, 6144
Z_MAIN_W = 7168
ZS_GATE = 0
ZS_SMALL = GATE_RANK
ZS_A_LANE = 0
ZS_F_LANE = GLA_GATE_RANK
ZS_W = GATE_RANK + LANES

VMEM_LIMIT = 56 * 1024 * 1024


def _cparams(sem):
    return pltpu.CompilerParams(dimension_semantics=sem, vmem_limit_bytes=VMEM_LIMIT)


def _dot(a, b, precision=None):
    return jnp.dot(a, b, preferred_element_type=F32, precision=precision)


def _dot_nt(a, b):
    return lax.dot_general(a, b, (((1,), (1,)), ((), ())), preferred_element_type=F32)


def _dot_tn(a, b):
    return lax.dot_general(a, b, (((0,), (0,)), ((), ())), preferred_element_type=F32)


def _log_sigmoid(x):
    return jnp.minimum(x, 0.0) - jnp.log(1.0 + jnp.exp(-jnp.abs(x)))


def _sigmoid(x):
    return 1.0 / (1.0 + jnp.exp(-x))


def _rms_rows(x, g):
    return x * lax.rsqrt(jnp.mean(x * x, axis=-1, keepdims=True) + EPS) * g


def _inproj_kernel(x_ref, g_ref, w_ref, ws_ref, z_ref, zs_ref, h_ref):
    @pl.when(pl.program_id(1) == 0)
    def _():
        h = _rms_rows(x_ref[...], g_ref[...]).astype(BF16)
        h_ref[...] = h
        zs_ref[...] = _dot(h, ws_ref[...])

    z_ref[...] = _dot(h_ref[...], w_ref[...]).astype(BF16)


def _inproj(x, g, w_main, w_small, layer, tm=512, tn=1024):
    T, D = x.shape
    n_main = w_main.shape[-1]
    n_small = w_small.shape[-1]
    return pl.pallas_call(
        _inproj_kernel,
        grid=(T // tm, n_main // tn),
        in_specs=[
            pl.BlockSpec((tm, D), lambda i, j: (i, 0)),
            pl.BlockSpec((None, 1, D), lambda i, j: (layer, 0, 0)),
            pl.BlockSpec((None, D, tn), lambda i, j: (layer, 0, j)),
            pl.BlockSpec((None, D, n_small), lambda i, j: (layer, 0, 0)),
        ],
        out_specs=[
            pl.BlockSpec((tm, tn), lambda i, j: (i, j)),
            pl.BlockSpec((tm, n_small), lambda i, j: (i, 0)),
        ],
        out_shape=[
            jax.ShapeDtypeStruct((T, n_main), BF16),
            jax.ShapeDtypeStruct((T, n_small), F32),
        ],
        scratch_shapes=[pltpu.VMEM((tm, D), BF16)],
        compiler_params=_cparams(("parallel", "arbitrary")),
        name="inproj",
    )(x, g, w_main, w_small)


def _pool_kernel(cur_ref, prev_ref, w_ref, s_ref, o_ref, ext_ref, *, tm, blocks_per_seq):
    blk = pl.program_id(0) % blocks_per_seq
    cur = cur_ref[...].astype(F32)
    ext_ref[0:POOL_HALO, :] = jnp.where(blk == 0, 0.0, prev_ref[...].astype(F32))
    ext_ref[POOL_HALO:, :] = cur
    pos = blk * tm + lax.broadcasted_iota(jnp.int32, (tm, 1), 0)
    for gi, win in enumerate(POOL_WINDOWS):
        lo = gi * POOL_GROUP
        u = cur[:, lo:lo + POOL_GROUP]
        acc = u
        for s in range(1, win):
            acc = acc + ext_ref[POOL_HALO - s:POOL_HALO - s + tm, lo:lo + POOL_GROUP]
        count = jnp.minimum(pos + 1, win).astype(F32)
        p = (acc / count - u).astype(BF16)
        y = _dot(p, w_ref[gi]) * s_ref[:, lo:lo + POOL_GROUP]
        o_ref[:, lo:lo + POOL_GROUP] = y.astype(BF16)


def _pool(z, w_pool, s_pool, layer, seq, tm=512):
    T = z.shape[0]
    halo_per_tile = tm // POOL_HALO
    return pl.pallas_call(
        functools.partial(_pool_kernel, tm=tm, blocks_per_seq=seq // tm),
        grid=(T // tm,),
        in_specs=[
            pl.BlockSpec((tm, POOL_WIDTH), lambda i: (i, Z_POOL // POOL_WIDTH)),
            pl.BlockSpec((POOL_HALO, POOL_WIDTH),
                         lambda i: (jnp.maximum(i * halo_per_tile - 1, 0), Z_POOL // POOL_WIDTH)),
            pl.BlockSpec((None, len(POOL_WINDOWS), POOL_GROUP, POOL_GROUP), lambda i: (layer, 0, 0, 0)),
            pl.BlockSpec((None, 1, POOL_WIDTH), lambda i: (layer, 0, 0)),
        ],
        out_specs=pl.BlockSpec((tm, POOL_WIDTH), lambda i: (i, 0)),
        out_shape=jax.ShapeDtypeStruct((T, POOL_WIDTH), BF16),
        scratch_shapes=[pltpu.VMEM((tm + POOL_HALO, POOL_WIDTH), F32)],
        compiler_params=_cparams(("parallel",)),
        name="pool",
    )(z, z, w_pool, s_pool)


def _fcum_kernel(zs_ref, bf_ref, qx_ref, kx_ref, carry_ref, *, tc, blocks_per_seq):
    @pl.when(pl.program_id(0) % blocks_per_seq == 0)
    def _():
        carry_ref[...] = jnp.zeros_like(carry_ref)

    lf = _log_sigmoid(zs_ref[...] + bf_ref[...])
    row = lax.broadcasted_iota(jnp.int32, (tc, tc), 0)
    col = lax.broadcasted_iota(jnp.int32, (tc, tc), 1)
    tril = (row >= col).astype(F32)
    c = _dot(tril, lf, precision=HIGHEST) + carry_ref[...]
    carry_ref[...] = c[tc - 1:tc, :]

    lane = lax.broadcasted_iota(jnp.int32, (tc, LANES), 1)
    ones_q = ((lane >= 3) & (lane < 6)).astype(F32)
    for h in range(FOX_HEADS):
        big = c[:, ZS_F_LANE + h:ZS_F_LANE + h + 1] * (FOX_DH ** 0.5)
        hi = big.astype(BF16).astype(F32)
        rest = big - hi
        mid = rest.astype(BF16).astype(F32)
        lo = rest - mid
        trip = jnp.where(lane % 3 == 0, hi, jnp.where(lane % 3 == 1, mid, lo))
        qx_ref[:, h * LANES:(h + 1) * LANES] = jnp.where(lane < 3, trip, ones_q).astype(BF16)
        kx_ref[:, h * LANES:(h + 1) * LANES] = jnp.where(
            lane < 3, 1.0, jnp.where(lane < 6, -trip, 0.0)).astype(BF16)


def _fcum(zs, bf_pad, layer, seq, tc=512):
    T = zs.shape[0]
    x_spec = pl.BlockSpec((tc, FOX_HEADS * LANES), lambda i: (i, 0))
    x_shape = jax.ShapeDtypeStruct((T, FOX_HEADS * LANES), BF16)
    return pl.pallas_call(
        functools.partial(_fcum_kernel, tc=tc, blocks_per_seq=seq // tc),
        grid=(T // tc,),
        in_specs=[
            pl.BlockSpec((tc, LANES), lambda i: (i, ZS_SMALL // LANES)),
            pl.BlockSpec((None, 1, LANES), lambda i: (layer, 0, 0)),
        ],
        out_specs=[x_spec, x_spec],
        out_shape=[x_shape, x_shape],
        scratch_shapes=[pltpu.VMEM((1, LANES), F32)],
        compiler_params=_cparams(("arbitrary",)),
        name="fcum",
    )(zs, bf_pad)


def _gla_kernel(q_ref, k_ref, v_ref, r_ref, a_ref, wa_ref, ba_ref, gg_ref, o_ref,
                st_ref, g_ref, *, tg, hp):
    @pl.when(pl.program_id(2) == 0)
    def _():
        st_ref[...] = jnp.zeros_like(st_ref)

    logits = _dot(a_ref[...].astype(BF16), wa_ref[...]) + ba_ref[...]
    g_ref[...] = _log_sigmoid(logits) * (1.0 / GLA_GATE_NORMALIZER)

    C, SB = GLA_CHUNK, GLA_SUB
    nsb = C // SB
    row = lax.broadcasted_iota(jnp.int32, (C, C), 0)
    col = lax.broadcasted_iota(jnp.int32, (C, C), 1)
    tril = (row >= col).astype(F32)
    sdiff = row // SB - col // SB
    lane_sb = lax.broadcasted_iota(jnp.int32, (SB, C), 1)
    row_sb = lax.broadcasted_iota(jnp.int32, (SB, C), 0)
    gain = gg_ref[...]

    def rows4(parts):
        return jnp.concatenate([jnp.broadcast_to(p, (SB, GLA_DK)) for p in parts], axis=0)

    def one_head(hh, r0):
        kcols = slice(hh * GLA_DK, (hh + 1) * GLA_DK)
        vcols = slice(hh * GLA_DV, (hh + 1) * GLA_DV)
        q = q_ref[pl.ds(r0, C), kcols].astype(F32) * (GLA_DK ** -0.5)
        k = k_ref[pl.ds(r0, C), kcols].astype(F32)
        v = v_ref[pl.ds(r0, C), vcols]
        g = g_ref[pl.ds(r0, C), kcols]
        b = _dot(tril, g, precision=HIGHEST)
        edge = [jnp.zeros((1, GLA_DK), F32)] + [b[SB * (i + 1) - 1:SB * (i + 1), :] for i in range(nsb)]
        zero = edge[0]
        rb = rows4(edge[0:nsb])
        rn = rows4(edge[1:nsb + 1])
        qt = q * jnp.exp(b - rb)
        kt = (k * jnp.exp(rn - b)).astype(BF16)
        a_off = jnp.where(sdiff == 1, _dot_nt(qt.astype(BF16), kt), 0.0)
        for s in range(2, nsb):
            d = rows4([zero] * s + [jnp.exp(edge[i] - edge[i - s + 1]) for i in range(s, nsb)])
            a_off = a_off + jnp.where(sdiff == s, _dot_nt((qt * d).astype(BF16), kt), 0.0)
        a_rows = []
        for sb in range(nsb):
            sl = slice(SB * sb, SB * (sb + 1))
            q_s, k_s, b_s = q[sl], k[sl], b[sl]
            acc = a_off[sl]
            for j in range(SB):
                e = jnp.exp(jnp.minimum(b_s - b_s[j:j + 1], 0.0))
                colv = jnp.sum(q_s * (k_s[j:j + 1] * e), axis=-1, keepdims=True)
                acc = jnp.where((lane_sb == SB * sb + j) & (row_sb >= j), colv, acc)
            a_rows.append(acc)
        a = jnp.concatenate(a_rows, axis=0).astype(BF16)

        st = st_ref[hh]
        o = _dot(a, v) + _dot_nt((q * jnp.exp(b)).astype(BF16), st.astype(BF16))
        b_last = edge[nsb]
        k_dec = (k * jnp.exp(b_last - b)).astype(BF16)
        st_ref[hh] = st * jnp.exp(b_last) + _dot_tn(v, k_dec)

        o = o * lax.rsqrt(jnp.mean(o * o, axis=-1, keepdims=True) + EPS) * gain
        r = r_ref[pl.ds(r0, C), vcols].astype(F32)
        o_ref[pl.ds(r0, C), vcols] = (o * (r * _sigmoid(r))).astype(BF16)

    def chunk(c, carry):
        r0 = pl.multiple_of(c * C, C)
        for hh in range(hp):
            one_head(hh, r0)
        return carry

    lax.fori_loop(0, tg // C, chunk, 0)


def _gla(z, zs, wa_t, b_alpha, g_gla, layer, batch, seq, tg=512, hp=4):
    T = z.shape[0]
    nb = seq // tg
    kw, vw = hp * GLA_DK, hp * GLA_DV
    qb, kb, vb, rb = Z_GQ // kw, Z_GK // kw, Z_GV // vw, Z_GR // vw
    return pl.pallas_call(
        functools.partial(_gla_kernel, tg=tg, hp=hp),
        grid=(batch, GLA_HEADS // hp, nb),
        in_specs=[
            pl.BlockSpec((tg, kw), lambda b, h, n: (b * nb + n, qb + h)),
            pl.BlockSpec((tg, kw), lambda b, h, n: (b * nb + n, kb + h)),
            pl.BlockSpec((tg, vw), lambda b, h, n: (b * nb + n, vb + h)),
            pl.BlockSpec((tg, vw), lambda b, h, n: (b * nb + n, rb + h)),
            pl.BlockSpec((tg, LANES), lambda b, h, n: (b * nb + n, ZS_SMALL // LANES)),
            pl.BlockSpec((None, LANES, kw), lambda b, h, n: (layer, 0, h)),
            pl.BlockSpec((None, 1, kw), lambda b, h, n: (layer, 0, h)),
            pl.BlockSpec((None, 1, GLA_DV), lambda b, h, n: (layer, 0, 0)),
        ],
        out_specs=pl.BlockSpec((tg, vw), lambda b, h, n: (b * nb + n, h)),
        out_shape=jax.ShapeDtypeStruct((T, GLA_V_W), BF16),
        scratch_shapes=[pltpu.VMEM((hp, GLA_DV, GLA_DK), F32), pltpu.VMEM((tg, kw), F32)],
        compiler_params=_cparams(("parallel", "parallel", "arbitrary")),
        name="gla",
    )(z, z, z, z, zs, wa_t, b_alpha, g_gla)


def _fox_kernel(qi_ref, ki_ref, q_ref, qx_ref, k_ref, kx_ref, v_ref, o_ref,
                m_ref, l_ref, acc_ref, qa_ref, s_ref, *, tq, tk, strip):
    t = pl.program_id(2)
    qi = qi_ref[t]
    ki = ki_ref[t]
    exp2_scale = (FOX_DH ** -0.5) * float(np.log2(np.e))

    @pl.when(ki == 0)
    def _():
        m_ref[...] = jnp.full_like(m_ref, -jnp.inf)
        l_ref[...] = jnp.zeros_like(l_ref)
        acc_ref[...] = jnp.zeros_like(acc_ref)
        qa_ref[:, 0:FOX_DH] = q_ref[...]
        qa_ref[:, FOX_DH:] = qx_ref[...]

    def step(diag):
        ka = jnp.concatenate([k_ref[...], kx_ref[...]], axis=1)
        v = v_ref[...]
        nstrip = tq // strip

        def width(c):
            return (c + 1) * strip if diag else tk

        def scores(c):
            s_ref[c % 2, :, :width(c)] = _dot_nt(qa_ref[c * strip:(c + 1) * strip, :], ka[:width(c)])

        scores(0)
        for c in range(nstrip):
            rows = slice(c * strip, (c + 1) * strip)
            ncol = width(c)
            if c + 1 < nstrip:
                scores(c + 1)
            s = s_ref[c % 2, :, :ncol]
            if diag:
                row = lax.broadcasted_iota(jnp.int32, (strip, ncol), 0) + c * strip
                col = lax.broadcasted_iota(jnp.int32, (strip, ncol), 1)
                s = jnp.where(col <= row, s, -jnp.inf)
            m_old = m_ref[rows, :]
            m_new = jnp.maximum(m_old, jnp.max(s, axis=-1, keepdims=True))
            p = jnp.exp2((s - m_new) * exp2_scale)
            alpha = jnp.exp2((m_old - m_new) * exp2_scale)
            l_ref[rows, :] = alpha * l_ref[rows, :] + jnp.sum(p, axis=-1, keepdims=True)
            acc_ref[rows, :] = alpha * acc_ref[rows, :] + _dot(p.astype(BF16), v[:ncol])
            m_ref[rows, :] = m_new

    @pl.when(ki < qi)
    def _():
        step(False)

    @pl.when(ki == qi)
    def _():
        step(True)
        o_ref[...] = (acc_ref[...] / l_ref[...]).astype(BF16)


def _fox(z, qx, kx, batch, seq, tq=1024, strip=256):
    T = z.shape[0]
    tk = tq
    nq = seq // tq
    pairs = [(qi, ki) for qi in range(nq) for ki in range(qi + 1)]
    qi_tab = jnp.asarray(np.array([p[0] for p in pairs], np.int32))
    ki_tab = jnp.asarray(np.array([p[1] for p in pairs], np.int32))
    qb, kb, vb = Z_FQ // FOX_DH, Z_FK // FOX_DH, Z_FV // FOX_DH
    grid_spec = pltpu.PrefetchScalarGridSpec(
        num_scalar_prefetch=2,
        grid=(batch, FOX_HEADS, len(pairs)),
        in_specs=[
            pl.BlockSpec((tq, FOX_DH), lambda b, h, t, qt, kt: (b * nq + qt[t], qb + h)),
            pl.BlockSpec((tq, LANES), lambda b, h, t, qt, kt: (b * nq + qt[t], h)),
            pl.BlockSpec((tk, FOX_DH), lambda b, h, t, qt, kt: (b * nq + kt[t], kb + h)),
            pl.BlockSpec((tk, LANES), lambda b, h, t, qt, kt: (b * nq + kt[t], h)),
            pl.BlockSpec((tk, FOX_DH), lambda b, h, t, qt, kt: (b * nq + kt[t], vb + h)),
        ],
        out_specs=pl.BlockSpec((tq, FOX_DH), lambda b, h, t, qt, kt: (b * nq + qt[t], h)),
        scratch_shapes=[
            pltpu.VMEM((tq, 1), F32), pltpu.VMEM((tq, 1), F32),
            pltpu.VMEM((tq, FOX_DH), F32), pltpu.VMEM((tq, FOX_DH + LANES), BF16),
            pltpu.VMEM((2, strip, tk), F32),
        ],
    )
    return pl.pallas_call(
        functools.partial(_fox_kernel, tq=tq, tk=tk, strip=strip),
        grid_spec=grid_spec,
        out_shape=jax.ShapeDtypeStruct((T, FOX_W), BF16),
        compiler_params=_cparams(("parallel", "parallel", "arbitrary")),
        name="fox",
    )(qi_tab, ki_tab, z, qx, z, kx, z)


def _merge_kernel(yp_ref, yg_ref, yf_ref, gl_ref, wp_ref, wg_ref, wf_ref,
                  u0_ref, u1_ref, u2_ref, b0_ref, b1_ref, b2_ref, o_ref):
    gl = gl_ref[...].astype(BF16)
    out = _sigmoid(_dot(gl, u0_ref[...]) + b0_ref[...]) * _dot(yp_ref[...], wp_ref[...])
    out = out + _sigmoid(_dot(gl, u1_ref[...]) + b1_ref[...]) * _dot(yg_ref[...], wg_ref[...])
    out = out + _sigmoid(_dot(gl, u2_ref[...]) + b2_ref[...]) * _dot(yf_ref[...], wf_ref[...])
    o_ref[...] = out.astype(BF16)


def _merge(yp, yg, yf, zs, wpp, wpg, wpf, wgu, b_gate, layer, tm=1024, tn=512):
    T = yp.shape[0]
    nj = D_MODEL // tn
    y_spec = pl.BlockSpec((tm, POOL_WIDTH), lambda i, j: (i, 0))
    w_spec = pl.BlockSpec((None, POOL_WIDTH, tn), lambda i, j: (layer, 0, j))

    def u_spec(br):
        return pl.BlockSpec((None, GATE_RANK, tn), lambda i, j: (layer, 0, br * nj + j))

    def b_spec(br):
        return pl.BlockSpec((None, 1, tn), lambda i, j: (layer, 0, br * nj + j))

    return pl.pallas_call(
        _merge_kernel,
        grid=(T // tm, nj),
        in_specs=[y_spec, y_spec, y_spec,
                  pl.BlockSpec((tm, GATE_RANK), lambda i, j: (i, ZS_GATE // GATE_RANK)),
                  w_spec, w_spec, w_spec,
                  u_spec(0), u_spec(1), u_spec(2), b_spec(0), b_spec(1), b_spec(2)],
        out_specs=pl.BlockSpec((tm, tn), lambda i, j: (i, j)),
        out_shape=jax.ShapeDtypeStruct((T, D_MODEL), BF16),
        compiler_params=_cparams(("parallel", "arbitrary")),
        name="merge",
    )(yp, yg, yf, zs, wpp, wpg, wpf, wgu, wgu, wgu, b_gate, b_gate, b_gate)


def _mm_res_kernel(a_ref, w_ref, res_ref, o_ref):
    @pl.when(pl.program_id(2) == 0)
    def _():
        o_ref[...] = res_ref[...] + _dot(a_ref[...], w_ref[...])

    @pl.when(pl.program_id(2) != 0)
    def _():
        o_ref[...] += _dot(a_ref[...], w_ref[...])


def _mm_res(a, w, res, layer, tm, tn, tk, name):
    T, K = a.shape
    N = w.shape[-1]
    return pl.pallas_call(
        _mm_res_kernel,
        grid=(T // tm, N // tn, K // tk),
        in_specs=[
            pl.BlockSpec((tm, tk), lambda i, j, k: (i, k)),
            pl.BlockSpec((None, tk, tn), lambda i, j, k: (layer, k, j)),
            pl.BlockSpec((tm, tn), lambda i, j, k: (i, j)),
        ],
        out_specs=pl.BlockSpec((tm, tn), lambda i, j, k: (i, j)),
        out_shape=jax.ShapeDtypeStruct((T, N), F32),
        compiler_params=_cparams(("parallel", "parallel", "arbitrary")),
        name=name,
    )(a, w, res)


def _ffn_up_kernel(x_ref, g_ref, wg_ref, wu_ref, o_ref, h_ref):
    @pl.when(pl.program_id(1) == 0)
    def _():
        h_ref[...] = _rms_rows(x_ref[...], g_ref[...]).astype(BF16)

    h = h_ref[...]
    gate = _dot(h, wg_ref[...])
    up = _dot(h, wu_ref[...])
    o_ref[...] = (gate * _sigmoid(gate) * up).astype(BF16)


def _ffn_up(x, g, wg, wu, layer, tm=512, tn=512):
    T, D = x.shape
    N = wg.shape[-1]
    w_spec = pl.BlockSpec((None, D, tn), lambda i, j: (layer, 0, j))
    return pl.pallas_call(
        _ffn_up_kernel,
        grid=(T // tm, pl.cdiv(N, tn)),
        in_specs=[
            pl.BlockSpec((tm, D), lambda i, j: (i, 0)),
            pl.BlockSpec((None, 1, D), lambda i, j: (layer, 0, 0)),
            w_spec, w_spec,
        ],
        out_specs=pl.BlockSpec((tm, tn), lambda i, j: (i, j)),
        out_shape=jax.ShapeDtypeStruct((T, N), BF16),
        scratch_shapes=[pltpu.VMEM((tm, D), BF16)],
        compiler_params=_cparams(("parallel", "arbitrary")),
        name="ffn_up",
    )(x, g, wg, wu)


def _norm_kernel(x_ref, g_ref, o_ref):
    o_ref[...] = _rms_rows(x_ref[...], g_ref[...])


def _final_norm(x, g, tm=512):
    T, D = x.shape
    return pl.pallas_call(
        _norm_kernel,
        grid=(T // tm,),
        in_specs=[pl.BlockSpec((tm, D), lambda i: (i, 0)), pl.BlockSpec((1, D), lambda i: (0, 0))],
        out_specs=pl.BlockSpec((tm, D), lambda i: (i, 0)),
        out_shape=jax.ShapeDtypeStruct((T, D), F32),
        compiler_params=_cparams(("parallel",)),
        name="final_norm",
    )(x, g)


def _split_offsets():
    sizes = (POOL_WIDTH, GLA_QK_W, GLA_QK_W, GLA_V_W, GLA_GATE_RANK, GLA_V_W,
             FOX_W, FOX_W, FOX_W, FOX_HEADS, GATE_RANK)
    offs = np.concatenate([[0], np.cumsum(sizes)])
    return [int(o) for o in offs]


def _prep_w_in(w_in):
    o = _split_offsets()
    a_lo, a_hi = o[4], o[5]
    f_lo, f_hi = o[9], o[10]
    gl_lo, gl_hi = o[10], o[11]
    main = jnp.concatenate([w_in[..., :a_lo], w_in[..., a_hi:f_lo]], axis=-1).astype(BF16)
    pad = jnp.zeros(w_in.shape[:-1] + (ZS_W - GATE_RANK - GLA_GATE_RANK - FOX_HEADS,), w_in.dtype)
    small = jnp.concatenate([w_in[..., gl_lo:gl_hi], w_in[..., a_lo:a_hi], w_in[..., f_lo:f_hi], pad],
                            axis=-1).astype(BF16)
    return main, small


def kernel(x, g_mix, w_in, w_pool, s_pool, w_alpha, b_alpha, g_gla, b_f, w_gate_up, b_gate,
           w_proj_pool, w_proj_gla, w_proj_fox, w_o, g_ffn, w_ffn_gate, w_ffn_up, w_ffn_down, g_final):
    B, S, D = x.shape
    T = B * S
    L = g_mix.shape[0]
    xf = x.reshape(T, D)

    w_main, w_small = _prep_w_in(w_in)
    w_pool_b = w_pool.astype(BF16)
    wa = jnp.pad(w_alpha, ((0, 0), (ZS_A_LANE, LANES - ZS_A_LANE - GLA_GATE_RANK), (0, 0))).astype(BF16)
    bf_pad = jnp.pad(b_f, ((0, 0), (ZS_F_LANE, LANES - ZS_F_LANE - FOX_HEADS)))[:, None, :]
    wgu = w_gate_up.astype(BF16)
    wpp, wpg, wpf = (w.astype(BF16) for w in (w_proj_pool, w_proj_gla, w_proj_fox))
    wo = w_o.astype(BF16)
    wfg, wfu, wfd = (w.astype(BF16) for w in (w_ffn_gate, w_ffn_up, w_ffn_down))
    g_mix3, g_ffn3 = g_mix[:, None, :], g_ffn[:, None, :]
    s_pool3, b_alpha3, g_gla3, b_gate3 = (a[:, None, :] for a in (s_pool, b_alpha, g_gla, b_gate))

    for l in range(L):
        z, zs = _inproj(xf, g_mix3, w_main, w_small, l)
        y_pool = _pool(z, w_pool_b, s_pool3, l, S)
        qx, kx = _fcum(zs, bf_pad, l, S)
        y_gla = _gla(z, zs, wa, b_alpha3, g_gla3, l, B, S)
        y_fox = _fox(z, qx, kx, B, S)
        merged = _merge(y_pool, y_gla, y_fox, zs, wpp, wpg, wpf, wgu, b_gate3, l)
        xf = _mm_res(merged, wo, xf, l, tm=1024, tn=512, tk=D_MODEL, name="out_proj")
        hid = _ffn_up(xf, g_ffn3, wfg, wfu, l)
        xf = _mm_res(hid, wfd, xf, l, tm=512, tn=512, tk=FFN_HIDDEN, name="ffn_down")
    return _final_norm(xf, g_final[None, :]).reshape(B, S, D)
```

```python
import functools

import jax
import jax.numpy as jnp
import numpy as np
from jax import lax
from jax.experimental import pallas as pl
from jax.experimental.pallas import tpu as pltpu

F32 = jnp.float32
BF16 = jnp.bfloat16
HIGHEST = lax.Precision.HIGHEST

D_MODEL = 4096
DEPTH = 4
EPS = 1e-6
POOL_WIDTH = 1024
POOL_WINDOWS = (2, 4, 8, 16)
POOL_GROUP = POOL_WIDTH // len(POOL_WINDOWS)
POOL_HALO = 16
GLA_HEADS = 4
GLA_DK = 128
GLA_DV = 256
GLA_QK_W = GLA_HEADS * GLA_DK
GLA_V_W = GLA_HEADS * GLA_DV
GLA_GATE_RANK = 16
GLA_GATE_NORMALIZER = 16.0
GLA_CHUNK = 64
GLA_SUB = 16
FOX_HEADS = 8
FOX_DH = 128
FOX_W = FOX_HEADS * FOX_DH
N_BRANCH = 3
GATE_RANK = 256
FFN_HIDDEN = ((8 * D_MODEL // 3 + 255) // 256) * 256

LANES = 128
Z_POOL, Z_GQ, Z_GK, Z_GV, Z_GR, Z_FQ, Z_FK, Z_FV = 0, 1024, 1536, 2048, 3072, 4096, 5120, 6144
Z_MAIN_W = 7168
ZS_GATE = 0
ZS_SMALL = GATE_RANK
ZS_A_LANE = 0
ZS_F_LANE = GLA_GATE_RANK
ZS_W = GATE_RANK + LANES

VMEM_LIMIT = 56 * 1024 * 1024


def _cparams(sem):
    return pltpu.CompilerParams(dimension_semantics=sem, vmem_limit_bytes=VMEM_LIMIT)


def _dot(a, b, precision=None):
    return jnp.dot(a, b, preferred_element_type=F32, precision=precision)


def _dot_nt(a, b):
    return lax.dot_general(a, b, (((1,), (1,)), ((), ())), preferred_element_type=F32)


def _dot_tn(a, b):
    return lax.dot_general(a, b, (((0,), (0,)), ((), ())), preferred_element_type=F32)


def _log_sigmoid(x):
    return jnp.minimum(x, 0.0) - jnp.log(1.0 + jnp.exp(-jnp.abs(x)))


def _sigmoid(x):
    return 1.0 / (1.0 + jnp.exp(-x))


def _rms_rows(x, g):
    return x * lax.rsqrt(jnp.mean(x * x, axis=-1, keepdims=True) + EPS) * g


def _row_rinv(ssq_ref):
    return lax.rsqrt(ssq_ref[:, 0:1] * (1.0 / D_MODEL) + EPS)


def _accumulate_ssq(ssq_ref, x, j):
    part = jnp.broadcast_to(jnp.sum(x * x, axis=-1, keepdims=True), ssq_ref.shape)

    @pl.when(j == 0)
    def _():
        ssq_ref[...] = part

    @pl.when(j != 0)
    def _():
        ssq_ref[...] += part


def _norm_prep_kernel(x_ref, g_ref, xg_ref, ssq_ref):
    x = x_ref[...]
    xg_ref[...] = (x * g_ref[...]).astype(BF16)
    ssq_ref[...] = jnp.broadcast_to(jnp.sum(x * x, axis=-1, keepdims=True), ssq_ref.shape)


def _norm_prep(x, g, layer, tm=512):
    T, D = x.shape
    return pl.pallas_call(
        _norm_prep_kernel,
        grid=(T // tm,),
        in_specs=[pl.BlockSpec((tm, D), lambda i: (i, 0)),
                  pl.BlockSpec((None, 1, D), lambda i: (layer, 0, 0))],
        out_specs=[pl.BlockSpec((tm, D), lambda i: (i, 0)),
                   pl.BlockSpec((tm, LANES), lambda i: (i, 0))],
        out_shape=[jax.ShapeDtypeStruct((T, D), BF16), jax.ShapeDtypeStruct((T, LANES), F32)],
        compiler_params=_cparams(("parallel",)),
        name="norm_prep",
    )(x, g)


def _inproj_kernel(xg_ref, ssq_ref, w_ref, ws_ref, z_ref, zs_ref):
    rinv = _row_rinv(ssq_ref)

    @pl.when(pl.program_id(1) == 0)
    def _():
        zs_ref[...] = _dot(xg_ref[...], ws_ref[...]) * rinv

    z_ref[...] = (_dot(xg_ref[...], w_ref[...]) * rinv).astype(BF16)


def _inproj(xg, ssq, w_main, w_small, layer, tm=1024, tn=1024):
    T, D = xg.shape
    n_main = w_main.shape[-1]
    n_small = w_small.shape[-1]
    return pl.pallas_call(
        _inproj_kernel,
        grid=(T // tm, n_main // tn),
        in_specs=[
            pl.BlockSpec((tm, D), lambda i, j: (i, 0)),
            pl.BlockSpec((tm, LANES), lambda i, j: (i, 0)),
            pl.BlockSpec((None, D, tn), lambda i, j: (layer, 0, j)),
            pl.BlockSpec((None, D, n_small), lambda i, j: (layer, 0, 0), pipeline_mode=pl.Buffered(1)),
        ],
        out_specs=[
            pl.BlockSpec((tm, tn), lambda i, j: (i, j)),
            pl.BlockSpec((tm, n_small), lambda i, j: (i, 0)),
        ],
        out_shape=[
            jax.ShapeDtypeStruct((T, n_main), BF16),
            jax.ShapeDtypeStruct((T, n_small), F32),
        ],
        compiler_params=_cparams(("parallel", "arbitrary")),
        name="inproj",
    )(xg, ssq, w_main, w_small)


def _pool_kernel(cur_ref, prev_ref, w_ref, s_ref, o_ref, ext_ref, *, tm, blocks_per_seq):
    blk = pl.program_id(0) % blocks_per_seq
    cur = cur_ref[...].astype(F32)
    ext_ref[0:POOL_HALO, :] = jnp.where(blk == 0, 0.0, prev_ref[...].astype(F32))
    ext_ref[POOL_HALO:, :] = cur
    pos = blk * tm + lax.broadcasted_iota(jnp.int32, (tm, 1), 0)
    for gi, win in enumerate(POOL_WINDOWS):
        lo = gi * POOL_GROUP
        u = cur[:, lo:lo + POOL_GROUP]
        acc = u
        for s in range(1, win):
            acc = acc + ext_ref[POOL_HALO - s:POOL_HALO - s + tm, lo:lo + POOL_GROUP]
        count = jnp.minimum(pos + 1, win).astype(F32)
        p = (acc / count - u).astype(BF16)
        y = _dot(p, w_ref[gi]) * s_ref[:, lo:lo + POOL_GROUP]
        o_ref[:, lo:lo + POOL_GROUP] = y.astype(BF16)


def _pool(z, w_pool, s_pool, layer, seq, tm=512):
    T = z.shape[0]
    halo_per_tile = tm // POOL_HALO
    return pl.pallas_call(
        functools.partial(_pool_kernel, tm=tm, blocks_per_seq=seq // tm),
        grid=(T // tm,),
        in_specs=[
            pl.BlockSpec((tm, POOL_WIDTH), lambda i: (i, Z_POOL // POOL_WIDTH)),
            pl.BlockSpec((POOL_HALO, POOL_WIDTH),
                         lambda i: (jnp.maximum(i * halo_per_tile - 1, 0), Z_POOL // POOL_WIDTH)),
            pl.BlockSpec((None, len(POOL_WINDOWS), POOL_GROUP, POOL_GROUP), lambda i: (layer, 0, 0, 0)),
            pl.BlockSpec((None, 1, POOL_WIDTH), lambda i: (layer, 0, 0)),
        ],
        out_specs=pl.BlockSpec((tm, POOL_WIDTH), lambda i: (i, 0)),
        out_shape=jax.ShapeDtypeStruct((T, POOL_WIDTH), BF16),
        scratch_shapes=[pltpu.VMEM((tm + POOL_HALO, POOL_WIDTH), F32)],
        compiler_params=_cparams(("parallel",)),
        name="pool",
    )(z, z, w_pool, s_pool)


def _fcum_kernel(zs_ref, bf_ref, qx_ref, kx_ref, carry_ref, *, tc, blocks_per_seq):
    @pl.when(pl.program_id(0) % blocks_per_seq == 0)
    def _():
        carry_ref[...] = jnp.zeros_like(carry_ref)

    lf = _log_sigmoid(zs_ref[...] + bf_ref[...])
    row = lax.broadcasted_iota(jnp.int32, (tc, tc), 0)
    col = lax.broadcasted_iota(jnp.int32, (tc, tc), 1)
    tril = (row >= col).astype(F32)
    c = _dot(tril, lf, precision=HIGHEST) + carry_ref[...]
    carry_ref[...] = c[tc - 1:tc, :]

    lane = lax.broadcasted_iota(jnp.int32, (tc, LANES), 1)
    ones_q = ((lane >= 3) & (lane < 6)).astype(F32)
    for h in range(FOX_HEADS):
        big = c[:, ZS_F_LANE + h:ZS_F_LANE + h + 1] * (FOX_DH ** 0.5)
        hi = big.astype(BF16).astype(F32)
        rest = big - hi
        mid = rest.astype(BF16).astype(F32)
        lo = rest - mid
        trip = jnp.where(lane % 3 == 0, hi, jnp.where(lane % 3 == 1, mid, lo))
        qx_ref[:, h * LANES:(h + 1) * LANES] = jnp.where(lane < 3, trip, ones_q).astype(BF16)
        kx_ref[:, h * LANES:(h + 1) * LANES] = jnp.where(
            lane < 3, 1.0, jnp.where(lane < 6, -trip, 0.0)).astype(BF16)


def _fcum(zs, bf_pad, layer, seq, tc=512):
    T = zs.shape[0]
    x_spec = pl.BlockSpec((tc, FOX_HEADS * LANES), lambda i: (i, 0))
    x_shape = jax.ShapeDtypeStruct((T, FOX_HEADS * LANES), BF16)
    return pl.pallas_call(
        functools.partial(_fcum_kernel, tc=tc, blocks_per_seq=seq // tc),
        grid=(T // tc,),
        in_specs=[
            pl.BlockSpec((tc, LANES), lambda i: (i, ZS_SMALL // LANES)),
            pl.BlockSpec((None, 1, LANES), lambda i: (layer, 0, 0)),
        ],
        out_specs=[x_spec, x_spec],
        out_shape=[x_shape, x_shape],
        scratch_shapes=[pltpu.VMEM((1, LANES), F32)],
        compiler_params=_cparams(("arbitrary",)),
        name="fcum",
    )(zs, bf_pad)


def _gla_kernel(q_ref, k_ref, v_ref, r_ref, a_ref, wa_ref, ba_ref, gg_ref, o_ref,
                st_ref, g_ref, *, tg, hp):
    @pl.when(pl.program_id(2) == 0)
    def _():
        st_ref[...] = jnp.zeros_like(st_ref)

    logits = _dot(a_ref[...].astype(BF16), wa_ref[...]) + ba_ref[...]
    g_ref[...] = _log_sigmoid(logits) * (1.0 / GLA_GATE_NORMALIZER)

    C, SB = GLA_CHUNK, GLA_SUB
    nsb = C // SB
    row = lax.broadcasted_iota(jnp.int32, (C, C), 0)
    col = lax.broadcasted_iota(jnp.int32, (C, C), 1)
    tril = (row >= col).astype(F32)
    sdiff = row // SB - col // SB
    lane_sb = lax.broadcasted_iota(jnp.int32, (SB, C), 1)
    row_sb = lax.broadcasted_iota(jnp.int32, (SB, C), 0)
    gain = gg_ref[...]

    def rows4(parts):
        return jnp.concatenate([jnp.broadcast_to(p, (SB, GLA_DK)) for p in parts], axis=0)

    def one_head(hh, r0):
        kcols = slice(hh * GLA_DK, (hh + 1) * GLA_DK)
        vcols = slice(hh * GLA_DV, (hh + 1) * GLA_DV)
        q = q_ref[pl.ds(r0, C), kcols].astype(F32) * (GLA_DK ** -0.5)
        k = k_ref[pl.ds(r0, C), kcols].astype(F32)
        v = v_ref[pl.ds(r0, C), vcols]
        g = g_ref[pl.ds(r0, C), kcols]
        b = _dot(tril, g, precision=HIGHEST)
        edge = [jnp.zeros((1, GLA_DK), F32)] + [b[SB * (i + 1) - 1:SB * (i + 1), :] for i in range(nsb)]
        zero = edge[0]
        rb = rows4(edge[0:nsb])
        rn = rows4(edge[1:nsb + 1])
        qt = q * jnp.exp(b - rb)
        kt = (k * jnp.exp(rn - b)).astype(BF16)
        a_off = jnp.where(sdiff == 1, _dot_nt(qt.astype(BF16), kt), 0.0)
        for s in range(2, nsb):
            d = rows4([zero] * s + [jnp.exp(edge[i] - edge[i - s + 1]) for i in range(s, nsb)])
            a_off = a_off + jnp.where(sdiff == s, _dot_nt((qt * d).astype(BF16), kt), 0.0)
        a_rows = []
        for sb in range(nsb):
            sl = slice(SB * sb, SB * (sb + 1))
            q_s, k_s, b_s = q[sl], k[sl], b[sl]
            acc = a_off[sl]
            for j in range(SB):
                e = jnp.exp(jnp.minimum(b_s - b_s[j:j + 1], 0.0))
                colv = jnp.sum(q_s * (k_s[j:j + 1] * e), axis=-1, keepdims=True)
                acc = jnp.where((lane_sb == SB * sb + j) & (row_sb >= j), colv, acc)
            a_rows.append(acc)
        a = jnp.concatenate(a_rows, axis=0).astype(BF16)

        st = st_ref[hh]
        o = _dot(a, v) + _dot_nt((q * jnp.exp(b)).astype(BF16), st.astype(BF16))
        b_last = edge[nsb]
        k_dec = (k * jnp.exp(b_last - b)).astype(BF16)
        st_ref[hh] = st * jnp.exp(b_last) + _dot_tn(v, k_dec)

        o = o * lax.rsqrt(jnp.mean(o * o, axis=-1, keepdims=True) + EPS) * gain
        r = r_ref[pl.ds(r0, C), vcols].astype(F32)
        o_ref[pl.ds(r0, C), vcols] = (o * (r * _sigmoid(r))).astype(BF16)

    def chunk(c, carry):
        r0 = pl.multiple_of(c * C, C)
        for hh in range(hp):
            one_head(hh, r0)
        return carry

    lax.fori_loop(0, tg // C, chunk, 0)


def _gla(z, zs, wa_t, b_alpha, g_gla, layer, batch, seq, tg=512, hp=4):
    T = z.shape[0]
    nb = seq // tg
    kw, vw = hp * GLA_DK, hp * GLA_DV
    qb, kb, vb, rb = Z_GQ // kw, Z_GK // kw, Z_GV // vw, Z_GR // vw
    return pl.pallas_call(
        functools.partial(_gla_kernel, tg=tg, hp=hp),
        grid=(batch, GLA_HEADS // hp, nb),
        in_specs=[
            pl.BlockSpec((tg, kw), lambda b, h, n: (b * nb + n, qb + h)),
            pl.BlockSpec((tg, kw), lambda b, h, n: (b * nb + n, kb + h)),
            pl.BlockSpec((tg, vw), lambda b, h, n: (b * nb + n, vb + h)),
            pl.BlockSpec((tg, vw), lambda b, h, n: (b * nb + n, rb + h)),
            pl.BlockSpec((tg, LANES), lambda b, h, n: (b * nb + n, ZS_SMALL // LANES)),
            pl.BlockSpec((None, LANES, kw), lambda b, h, n: (layer, 0, h)),
            pl.BlockSpec((None, 1, kw), lambda b, h, n: (layer, 0, h)),
            pl.BlockSpec((None, 1, GLA_DV), lambda b, h, n: (layer, 0, 0)),
        ],
        out_specs=pl.BlockSpec((tg, vw), lambda b, h, n: (b * nb + n, h)),
        out_shape=jax.ShapeDtypeStruct((T, GLA_V_W), BF16),
        scratch_shapes=[pltpu.VMEM((hp, GLA_DV, GLA_DK), F32), pltpu.VMEM((tg, kw), F32)],
        compiler_params=_cparams(("parallel", "parallel", "arbitrary")),
        name="gla",
    )(z, z, z, z, zs, wa_t, b_alpha, g_gla)


def _fox_kernel(qi_ref, ki_ref, q_ref, qx_ref, k_ref, kx_ref, v_ref, o_ref,
                m_ref, l_ref, acc_ref, qa_ref, s_ref, *, tq, tk, strip):
    t = pl.program_id(2)
    qi = qi_ref[t]
    ki = ki_ref[t]
    exp2_scale = (FOX_DH ** -0.5) * float(np.log2(np.e))

    @pl.when(ki == 0)
    def _():
        m_ref[...] = jnp.full_like(m_ref, -jnp.inf)
        l_ref[...] = jnp.zeros_like(l_ref)
        acc_ref[...] = jnp.zeros_like(acc_ref)
        qa_ref[:, 0:FOX_DH] = q_ref[...]
        qa_ref[:, FOX_DH:] = qx_ref[...]

    def step(diag):
        ka = jnp.concatenate([k_ref[...], kx_ref[...]], axis=1)
        v = v_ref[...]
        nstrip = tq // strip

        def width(c):
            return (c + 1) * strip if diag else tk

        def scores(c):
            s_ref[c % 2, :, :width(c)] = _dot_nt(qa_ref[c * strip:(c + 1) * strip, :], ka[:width(c)])

        scores(0)
        for c in range(nstrip):
            rows = slice(c * strip, (c + 1) * strip)
            ncol = width(c)
            if c + 1 < nstrip:
                scores(c + 1)
            s = s_ref[c % 2, :, :ncol]
            if diag:
                row = lax.broadcasted_iota(jnp.int32, (strip, ncol), 0) + c * strip
                col = lax.broadcasted_iota(jnp.int32, (strip, ncol), 1)
                s = jnp.where(col <= row, s, -jnp.inf)
            m_old = m_ref[rows, :]
            m_new = jnp.maximum(m_old, jnp.max(s, axis=-1, keepdims=True))
            p = jnp.exp2((s - m_new) * exp2_scale)
            alpha = jnp.exp2((m_old - m_new) * exp2_scale)
            l_ref[rows, :] = alpha * l_ref[rows, :] + jnp.sum(p, axis=-1, keepdims=True)
            acc_ref[rows, :] = alpha * acc_ref[rows, :] + _dot(p.astype(BF16), v[:ncol])
            m_ref[rows, :] = m_new

    @pl.when(ki < qi)
    def _():
        step(False)

    @pl.when(ki == qi)
    def _():
        step(True)
        o_ref[...] = (acc_ref[...] / l_ref[...]).astype(BF16)


def _fox(z, qx, kx, batch, seq, tq=2048, strip=256):
    T = z.shape[0]
    tk = tq
    nq = seq // tq
    pairs = [(qi, ki) for qi in range(nq) for ki in range(qi + 1)]
    qi_tab = jnp.asarray(np.array([p[0] for p in pairs], np.int32))
    ki_tab = jnp.asarray(np.array([p[1] for p in pairs], np.int32))
    qb, kb, vb = Z_FQ // FOX_DH, Z_FK // FOX_DH, Z_FV // FOX_DH
    grid_spec = pltpu.PrefetchScalarGridSpec(
        num_scalar_prefetch=2,
        grid=(batch, FOX_HEADS, len(pairs)),
        in_specs=[
            pl.BlockSpec((tq, FOX_DH), lambda b, h, t, qt, kt: (b * nq + qt[t], qb + h)),
            pl.BlockSpec((tq, LANES), lambda b, h, t, qt, kt: (b * nq + qt[t], h)),
            pl.BlockSpec((tk, FOX_DH), lambda b, h, t, qt, kt: (b * nq + kt[t], kb + h)),
            pl.BlockSpec((tk, LANES), lambda b, h, t, qt, kt: (b * nq + kt[t], h)),
            pl.BlockSpec((tk, FOX_DH), lambda b, h, t, qt, kt: (b * nq + kt[t], vb + h)),
        ],
        out_specs=pl.BlockSpec((tq, FOX_DH), lambda b, h, t, qt, kt: (b * nq + qt[t], h)),
        scratch_shapes=[
            pltpu.VMEM((tq, 1), F32), pltpu.VMEM((tq, 1), F32),
            pltpu.VMEM((tq, FOX_DH), F32), pltpu.VMEM((tq, FOX_DH + LANES), BF16),
            pltpu.VMEM((2, strip, tk), F32),
        ],
    )
    return pl.pallas_call(
        functools.partial(_fox_kernel, tq=tq, tk=tk, strip=strip),
        grid_spec=grid_spec,
        out_shape=jax.ShapeDtypeStruct((T, FOX_W), BF16),
        compiler_params=_cparams(("parallel", "parallel", "arbitrary")),
        name="fox",
    )(qi_tab, ki_tab, z, qx, z, kx, z)


def _merge_kernel(yp_ref, yg_ref, yf_ref, gl_ref, wp_ref, wg_ref, wf_ref,
                  u0_ref, u1_ref, u2_ref, b0_ref, b1_ref, b2_ref, o_ref):
    gl = gl_ref[...].astype(BF16)
    out = _sigmoid(_dot(gl, u0_ref[...]) + b0_ref[...]) * _dot(yp_ref[...], wp_ref[...])
    out = out + _sigmoid(_dot(gl, u1_ref[...]) + b1_ref[...]) * _dot(yg_ref[...], wg_ref[...])
    out = out + _sigmoid(_dot(gl, u2_ref[...]) + b2_ref[...]) * _dot(yf_ref[...], wf_ref[...])
    o_ref[...] = out.astype(BF16)


def _merge(yp, yg, yf, zs, wpp, wpg, wpf, wgu, b_gate, layer, tm=1024, tn=512):
    T = yp.shape[0]
    nj = D_MODEL // tn
    y_spec = pl.BlockSpec((tm, POOL_WIDTH), lambda i, j: (i, 0))
    w_spec = pl.BlockSpec((None, POOL_WIDTH, tn), lambda i, j: (layer, 0, j))

    def u_spec(br):
        return pl.BlockSpec((None, GATE_RANK, tn), lambda i, j: (layer, 0, br * nj + j))

    def b_spec(br):
        return pl.BlockSpec((None, 1, tn), lambda i, j: (layer, 0, br * nj + j))

    return pl.pallas_call(
        _merge_kernel,
        grid=(T // tm, nj),
        in_specs=[y_spec, y_spec, y_spec,
                  pl.BlockSpec((tm, GATE_RANK), lambda i, j: (i, ZS_GATE // GATE_RANK)),
                  w_spec, w_spec, w_spec,
                  u_spec(0), u_spec(1), u_spec(2), b_spec(0), b_spec(1), b_spec(2)],
        out_specs=pl.BlockSpec((tm, tn), lambda i, j: (i, j)),
        out_shape=jax.ShapeDtypeStruct((T, D_MODEL), BF16),
        compiler_params=_cparams(("parallel", "arbitrary")),
        name="merge",
    )(yp, yg, yf, zs, wpp, wpg, wpf, wgu, wgu, wgu, b_gate, b_gate, b_gate)


def _mm_res_kernel(a_ref, w_ref, res_ref, o_ref):
    o_ref[...] = res_ref[...] + _dot(a_ref[...], w_ref[...])


def _mm_res_norm_kernel(a_ref, w_ref, res_ref, g_ref, o_ref, xg_ref, ssq_ref):
    x = res_ref[...] + _dot(a_ref[...], w_ref[...])
    o_ref[...] = x
    xg_ref[...] = (x * g_ref[...]).astype(BF16)
    _accumulate_ssq(ssq_ref, x, pl.program_id(1))


def _mm_res(a, w, res, layer, tm, tn, name, gain=None, gain_layer=None):
    T, K = a.shape
    N = w.shape[-1]
    in_specs = [
        pl.BlockSpec((tm, K), lambda i, j: (i, 0)),
        pl.BlockSpec((None, K, tn), lambda i, j: (layer, 0, j)),
        pl.BlockSpec((tm, tn), lambda i, j: (i, j)),
    ]
    x_spec = pl.BlockSpec((tm, tn), lambda i, j: (i, j))
    x_shape = jax.ShapeDtypeStruct((T, N), F32)
    if gain is None:
        return pl.pallas_call(
            _mm_res_kernel, grid=(T // tm, N // tn), in_specs=in_specs, out_specs=x_spec,
            out_shape=x_shape, compiler_params=_cparams(("parallel", "arbitrary")), name=name,
        )(a, w, res)
    return pl.pallas_call(
        _mm_res_norm_kernel,
        grid=(T // tm, N // tn),
        in_specs=in_specs + [pl.BlockSpec((None, 1, tn), lambda i, j: (gain_layer, 0, j))],
        out_specs=[x_spec, x_spec, pl.BlockSpec((tm, LANES), lambda i, j: (i, 0))],
        out_shape=[x_shape, jax.ShapeDtypeStruct((T, N), BF16), jax.ShapeDtypeStruct((T, LANES), F32)],
        compiler_params=_cparams(("parallel", "arbitrary")),
        name=name,
    )(a, w, res, gain)


def _ffn_up_kernel(xg_ref, ssq_ref, wg_ref, wu_ref, o_ref):
    rinv = _row_rinv(ssq_ref)
    xg = xg_ref[...]
    gate = _dot(xg, wg_ref[...]) * rinv
    up = _dot(xg, wu_ref[...]) * rinv
    o_ref[...] = (gate * _sigmoid(gate) * up).astype(BF16)


def _ffn_up(xg, ssq, wg, wu, layer, tm=1024, tn=512):
    T, D = xg.shape
    N = wg.shape[-1]
    w_spec = pl.BlockSpec((None, D, tn), lambda i, j: (layer, 0, j))
    return pl.pallas_call(
        _ffn_up_kernel,
        grid=(T // tm, pl.cdiv(N, tn)),
        in_specs=[
            pl.BlockSpec((tm, D), lambda i, j: (i, 0)),
            pl.BlockSpec((tm, LANES), lambda i, j: (i, 0)),
            w_spec, w_spec,
        ],
        out_specs=pl.BlockSpec((tm, tn), lambda i, j: (i, j)),
        out_shape=jax.ShapeDtypeStruct((T, N), BF16),
        compiler_params=_cparams(("parallel", "arbitrary")),
        name="ffn_up",
    )(xg, ssq, wg, wu)


def _norm_kernel(x_ref, g_ref, o_ref):
    o_ref[...] = _rms_rows(x_ref[...], g_ref[...])


def _final_norm(x, g, tm=512):
    T, D = x.shape
    return pl.pallas_call(
        _norm_kernel,
        grid=(T // tm,),
        in_specs=[pl.BlockSpec((tm, D), lambda i: (i, 0)), pl.BlockSpec((1, D), lambda i: (0, 0))],
        out_specs=pl.BlockSpec((tm, D), lambda i: (i, 0)),
        out_shape=jax.ShapeDtypeStruct((T, D), F32),
        compiler_params=_cparams(("parallel",)),
        name="final_norm",
    )(x, g)


def _split_offsets():
    sizes = (POOL_WIDTH, GLA_QK_W, GLA_QK_W, GLA_V_W, GLA_GATE_RANK, GLA_V_W,
             FOX_W, FOX_W, FOX_W, FOX_HEADS, GATE_RANK)
    offs = np.concatenate([[0], np.cumsum(sizes)])
    return [int(o) for o in offs]


def _prep_w_in(w_in):
    o = _split_offsets()
    a_lo, a_hi = o[4], o[5]
    f_lo, f_hi = o[9], o[10]
    gl_lo, gl_hi = o[10], o[11]
    main = jnp.concatenate([w_in[..., :a_lo], w_in[..., a_hi:f_lo]], axis=-1).astype(BF16)
    pad = jnp.zeros(w_in.shape[:-1] + (ZS_W - GATE_RANK - GLA_GATE_RANK - FOX_HEADS,), w_in.dtype)
    small = jnp.concatenate([w_in[..., gl_lo:gl_hi], w_in[..., a_lo:a_hi], w_in[..., f_lo:f_hi], pad],
                            axis=-1).astype(BF16)
    return main, small


def kernel(x, g_mix, w_in, w_pool, s_pool, w_alpha, b_alpha, g_gla, b_f, w_gate_up, b_gate,
           w_proj_pool, w_proj_gla, w_proj_fox, w_o, g_ffn, w_ffn_gate, w_ffn_up, w_ffn_down, g_final):
    B, S, D = x.shape
    T = B * S
    L = g_mix.shape[0]
    xf = x.reshape(T, D)

    w_main, w_small = _prep_w_in(w_in)
    w_pool_b = w_pool.astype(BF16)
    wa = jnp.pad(w_alpha, ((0, 0), (ZS_A_LANE, LANES - ZS_A_LANE - GLA_GATE_RANK), (0, 0))).astype(BF16)
    bf_pad = jnp.pad(b_f, ((0, 0), (ZS_F_LANE, LANES - ZS_F_LANE - FOX_HEADS)))[:, None, :]
    wgu = w_gate_up.astype(BF16)
    wpp, wpg, wpf = (w.astype(BF16) for w in (w_proj_pool, w_proj_gla, w_proj_fox))
    wo = w_o.astype(BF16)
    wfg, wfu, wfd = (w.astype(BF16) for w in (w_ffn_gate, w_ffn_up, w_ffn_down))
    g_mix3, g_ffn3 = g_mix[:, None, :], g_ffn[:, None, :]
    s_pool3, b_alpha3, g_gla3, b_gate3 = (a[:, None, :] for a in (s_pool, b_alpha, g_gla, b_gate))

    xg, ssq = _norm_prep(xf, g_mix3, 0)
    for l in range(L):
        z, zs = _inproj(xg, ssq, w_main, w_small, l)
        y_pool = _pool(z, w_pool_b, s_pool3, l, S)
        qx, kx = _fcum(zs, bf_pad, l, S)
        y_gla = _gla(z, zs, wa, b_alpha3, g_gla3, l, B, S)
        y_fox = _fox(z, qx, kx, B, S)
        merged = _merge(y_pool, y_gla, y_fox, zs, wpp, wpg, wpf, wgu, b_gate3, l)
        xf, xg, ssq = _mm_res(merged, wo, xf, l, tm=1024, tn=512, name="out_proj", gain=g_ffn3, gain_layer=l)
        hid = _ffn_up(xg, ssq, wfg, wfu, l)
        if l + 1 < L:
            xf, xg, ssq = _mm_res(hid, wfd, xf, l, tm=512, tn=512, name="ffn_down", gain=g_mix3, gain_layer=l + 1)
        else:
            xf = _mm_res(hid, wfd, xf, l, tm=512, tn=512, name="ffn_down")
    return _final_norm(xf, g_final[None, :]).reshape(B, S, D)
```

```python
import functools

import jax
import jax.numpy as jnp
import numpy as np
from jax import lax
from jax.experimental import pallas as pl
from jax.experimental.pallas import tpu as pltpu

F32 = jnp.float32
BF16 = jnp.bfloat16
HIGHEST = lax.Precision.HIGHEST

D_MODEL = 4096
DEPTH = 4
EPS = 1e-6
POOL_WIDTH = 1024
POOL_WINDOWS = (2, 4, 8, 16)
POOL_GROUP = POOL_WIDTH // len(POOL_WINDOWS)
POOL_HALO = 16
GLA_HEADS = 4
GLA_DK = 128
GLA_DV = 256
GLA_QK_W = GLA_HEADS * GLA_DK
GLA_V_W = GLA_HEADS * GLA_DV
GLA_GATE_RANK = 16
GLA_GATE_NORMALIZER = 16.0
GLA_CHUNK = 64
GLA_SUB = 16
FOX_HEADS = 8
FOX_DH = 128
FOX_W = FOX_HEADS * FOX_DH
N_BRANCH = 3
GATE_RANK = 256
FFN_HIDDEN = ((8 * D_MODEL // 3 + 255) // 256) * 256

LANES = 128
Z_POOL, Z_GQ, Z_GK, Z_GV, Z_GR, Z_FQ, Z_FK, Z_FV = 0, 1024, 1536, 2048, 3072, 4096, 5120, 6144
Z_MAIN_W = 7168
ZS_GATE = 0
ZS_SMALL = GATE_RANK
ZS_A_LANE = 0
ZS_F_LANE = GLA_GATE_RANK
ZS_W = GATE_RANK + LANES

VMEM_LIMIT = 56 * 1024 * 1024


def _cparams(sem):
    return pltpu.CompilerParams(dimension_semantics=sem, vmem_limit_bytes=VMEM_LIMIT)


def _dot(a, b, precision=None):
    return jnp.dot(a, b, preferred_element_type=F32, precision=precision)


def _dot_nt(a, b):
    return lax.dot_general(a, b, (((1,), (1,)), ((), ())), preferred_element_type=F32)


def _dot_tn(a, b):
    return lax.dot_general(a, b, (((0,), (0,)), ((), ())), preferred_element_type=F32)


def _log_sigmoid(x):
    return jnp.minimum(x, 0.0) - jnp.log(1.0 + jnp.exp(-jnp.abs(x)))


def _sigmoid(x):
    return 1.0 / (1.0 + jnp.exp(-x))


def _rms_rows(x, g):
    return x * lax.rsqrt(jnp.mean(x * x, axis=-1, keepdims=True) + EPS) * g


def _row_rinv(ssq_ref):
    return lax.rsqrt(ssq_ref[:, 0:1] * (1.0 / D_MODEL) + EPS)


def _accumulate_ssq(ssq_ref, x, j):
    part = jnp.broadcast_to(jnp.sum(x * x, axis=-1, keepdims=True), ssq_ref.shape)

    @pl.when(j == 0)
    def _():
        ssq_ref[...] = part

    @pl.when(j != 0)
    def _():
        ssq_ref[...] += part


def _norm_prep_kernel(x_ref, g_ref, xg_ref, ssq_ref):
    x = x_ref[...]
    xg_ref[...] = (x * g_ref[...]).astype(BF16)
    ssq_ref[...] = jnp.broadcast_to(jnp.sum(x * x, axis=-1, keepdims=True), ssq_ref.shape)


def _norm_prep(x, g, layer, tm=512):
    T, D = x.shape
    return pl.pallas_call(
        _norm_prep_kernel,
        grid=(T // tm,),
        in_specs=[pl.BlockSpec((tm, D), lambda i: (i, 0)),
                  pl.BlockSpec((None, 1, D), lambda i: (layer, 0, 0))],
        out_specs=[pl.BlockSpec((tm, D), lambda i: (i, 0)),
                   pl.BlockSpec((tm, LANES), lambda i: (i, 0))],
        out_shape=[jax.ShapeDtypeStruct((T, D), BF16), jax.ShapeDtypeStruct((T, LANES), F32)],
        compiler_params=_cparams(("parallel",)),
        name="norm_prep",
    )(x, g)


def _inproj_kernel(xg_ref, ssq_ref, w_ref, ws_ref, z_ref, zs_ref):
    rinv = _row_rinv(ssq_ref)

    @pl.when(pl.program_id(1) == 0)
    def _():
        zs_ref[...] = _dot(xg_ref[...], ws_ref[...]) * rinv

    z_ref[...] = (_dot(xg_ref[...], w_ref[...]) * rinv).astype(BF16)


def _inproj(xg, ssq, w_main, w_small, layer, tm=1024, tn=1024):
    T, D = xg.shape
    n_main = w_main.shape[-1]
    n_small = w_small.shape[-1]
    return pl.pallas_call(
        _inproj_kernel,
        grid=(T // tm, n_main // tn),
        in_specs=[
            pl.BlockSpec((tm, D), lambda i, j: (i, 0)),
            pl.BlockSpec((tm, LANES), lambda i, j: (i, 0)),
            pl.BlockSpec((None, D, tn), lambda i, j: (layer, 0, j)),
            pl.BlockSpec((None, D, n_small), lambda i, j: (layer, 0, 0), pipeline_mode=pl.Buffered(1)),
        ],
        out_specs=[
            pl.BlockSpec((tm, tn), lambda i, j: (i, j)),
            pl.BlockSpec((tm, n_small), lambda i, j: (i, 0)),
        ],
        out_shape=[
            jax.ShapeDtypeStruct((T, n_main), BF16),
            jax.ShapeDtypeStruct((T, n_small), F32),
        ],
        compiler_params=_cparams(("parallel", "arbitrary")),
        name="inproj",
    )(xg, ssq, w_main, w_small)


def _pool_kernel(cur_ref, prev_ref, w_ref, s_ref, o_ref, ext_ref, *, tm, blocks_per_seq):
    blk = pl.program_id(0) % blocks_per_seq
    cur = cur_ref[...].astype(F32)
    ext_ref[0:POOL_HALO, :] = jnp.where(blk == 0, 0.0, prev_ref[...].astype(F32))
    ext_ref[POOL_HALO:, :] = cur
    pos = blk * tm + lax.broadcasted_iota(jnp.int32, (tm, 1), 0)
    for gi, win in enumerate(POOL_WINDOWS):
        lo = gi * POOL_GROUP
        u = cur[:, lo:lo + POOL_GROUP]
        acc = u
        for s in range(1, win):
            acc = acc + ext_ref[POOL_HALO - s:POOL_HALO - s + tm, lo:lo + POOL_GROUP]
        count = jnp.minimum(pos + 1, win).astype(F32)
        p = (acc / count - u).astype(BF16)
        y = _dot(p, w_ref[gi]) * s_ref[:, lo:lo + POOL_GROUP]
        o_ref[:, lo:lo + POOL_GROUP] = y.astype(BF16)


def _pool(z, w_pool, s_pool, layer, seq, tm=512):
    T = z.shape[0]
    halo_per_tile = tm // POOL_HALO
    return pl.pallas_call(
        functools.partial(_pool_kernel, tm=tm, blocks_per_seq=seq // tm),
        grid=(T // tm,),
        in_specs=[
            pl.BlockSpec((tm, POOL_WIDTH), lambda i: (i, Z_POOL // POOL_WIDTH)),
            pl.BlockSpec((POOL_HALO, POOL_WIDTH),
                         lambda i: (jnp.maximum(i * halo_per_tile - 1, 0), Z_POOL // POOL_WIDTH)),
            pl.BlockSpec((None, len(POOL_WINDOWS), POOL_GROUP, POOL_GROUP), lambda i: (layer, 0, 0, 0)),
            pl.BlockSpec((None, 1, POOL_WIDTH), lambda i: (layer, 0, 0)),
        ],
        out_specs=pl.BlockSpec((tm, POOL_WIDTH), lambda i: (i, 0)),
        out_shape=jax.ShapeDtypeStruct((T, POOL_WIDTH), BF16),
        scratch_shapes=[pltpu.VMEM((tm + POOL_HALO, POOL_WIDTH), F32)],
        compiler_params=_cparams(("parallel",)),
        name="pool",
    )(z, z, w_pool, s_pool)


def _fcum_kernel(zs_ref, bf_ref, qx_ref, kx_ref, carry_ref, *, tc, blocks_per_seq):
    @pl.when(pl.program_id(0) % blocks_per_seq == 0)
    def _():
        carry_ref[...] = jnp.zeros_like(carry_ref)

    lf = _log_sigmoid(zs_ref[...] + bf_ref[...])
    row = lax.broadcasted_iota(jnp.int32, (tc, tc), 0)
    col = lax.broadcasted_iota(jnp.int32, (tc, tc), 1)
    tril = (row >= col).astype(F32)
    c = _dot(tril, lf, precision=HIGHEST) + carry_ref[...]
    carry_ref[...] = c[tc - 1:tc, :]

    lane = lax.broadcasted_iota(jnp.int32, (tc, LANES), 1)
    ones_q = ((lane >= 3) & (lane < 6)).astype(F32)
    for h in range(FOX_HEADS):
        big = c[:, ZS_F_LANE + h:ZS_F_LANE + h + 1] * (FOX_DH ** 0.5)
        hi = big.astype(BF16).astype(F32)
        rest = big - hi
        mid = rest.astype(BF16).astype(F32)
        lo = rest - mid
        trip = jnp.where(lane % 3 == 0, hi, jnp.where(lane % 3 == 1, mid, lo))
        qx_ref[:, h * LANES:(h + 1) * LANES] = jnp.where(lane < 3, trip, ones_q).astype(BF16)
        kx_ref[:, h * LANES:(h + 1) * LANES] = jnp.where(
            lane < 3, 1.0, jnp.where(lane < 6, -trip, 0.0)).astype(BF16)


def _fcum(zs, bf_pad, layer, seq, tc=512):
    T = zs.shape[0]
    x_spec = pl.BlockSpec((tc, FOX_HEADS * LANES), lambda i: (i, 0))
    x_shape = jax.ShapeDtypeStruct((T, FOX_HEADS * LANES), BF16)
    return pl.pallas_call(
        functools.partial(_fcum_kernel, tc=tc, blocks_per_seq=seq // tc),
        grid=(T // tc,),
        in_specs=[
            pl.BlockSpec((tc, LANES), lambda i: (i, ZS_SMALL // LANES)),
            pl.BlockSpec((None, 1, LANES), lambda i: (layer, 0, 0)),
        ],
        out_specs=[x_spec, x_spec],
        out_shape=[x_shape, x_shape],
        scratch_shapes=[pltpu.VMEM((1, LANES), F32)],
        compiler_params=_cparams(("arbitrary",)),
        name="fcum",
    )(zs, bf_pad)


def _gla_kernel(q_ref, k_ref, v_ref, r_ref, a_ref, wa_ref, ba_ref, gg_ref, o_ref,
                st_ref, g_ref, *, tg, hp):
    @pl.when(pl.program_id(2) == 0)
    def _():
        st_ref[...] = jnp.zeros_like(st_ref)

    logits = _dot(a_ref[...].astype(BF16), wa_ref[...]) + ba_ref[...]
    g_ref[...] = _log_sigmoid(logits) * (1.0 / GLA_GATE_NORMALIZER)

    C, SB = GLA_CHUNK, GLA_SUB
    nsb = C // SB
    row = lax.broadcasted_iota(jnp.int32, (C, C), 0)
    col = lax.broadcasted_iota(jnp.int32, (C, C), 1)
    tril = (row >= col).astype(F32)
    sdiff = row // SB - col // SB
    lane_sb = lax.broadcasted_iota(jnp.int32, (SB, C), 1)
    row_sb = lax.broadcasted_iota(jnp.int32, (SB, C), 0)
    gain = gg_ref[...]

    def rows4(parts):
        return jnp.concatenate([jnp.broadcast_to(p, (SB, GLA_DK)) for p in parts], axis=0)

    def one_head(hh, r0):
        kcols = slice(hh * GLA_DK, (hh + 1) * GLA_DK)
        vcols = slice(hh * GLA_DV, (hh + 1) * GLA_DV)
        q = q_ref[pl.ds(r0, C), kcols].astype(F32) * (GLA_DK ** -0.5)
        k = k_ref[pl.ds(r0, C), kcols].astype(F32)
        v = v_ref[pl.ds(r0, C), vcols]
        g = g_ref[pl.ds(r0, C), kcols]
        b = _dot(tril, g, precision=HIGHEST)
        edge = [jnp.zeros((1, GLA_DK), F32)] + [b[SB * (i + 1) - 1:SB * (i + 1), :] for i in range(nsb)]
        zero = edge[0]
        rb = rows4(edge[0:nsb])
        rn = rows4(edge[1:nsb + 1])
        qt = q * jnp.exp(b - rb)
        kt = (k * jnp.exp(rn - b)).astype(BF16)
        a_off = jnp.where(sdiff == 1, _dot_nt(qt.astype(BF16), kt), 0.0)
        for s in range(2, nsb):
            d = rows4([zero] * s + [jnp.exp(edge[i] - edge[i - s + 1]) for i in range(s, nsb)])
            a_off = a_off + jnp.where(sdiff == s, _dot_nt((qt * d).astype(BF16), kt), 0.0)
        a_rows = []
        b2 = b * float(np.log2(np.e))
        for sb in range(nsb):
            sl = slice(SB * sb, SB * (sb + 1))
            q_s, k_s, b_s = q[sl], k[sl], b2[sl]
            acc = a_off[sl]
            for j in range(SB):
                e = jnp.exp2(b_s - b_s[j:j + 1])
                colv = jnp.sum(q_s * (k_s[j:j + 1] * e), axis=-1, keepdims=True)
                acc = jnp.where((lane_sb == SB * sb + j) & (row_sb >= j), colv, acc)
            a_rows.append(acc)
        a = jnp.concatenate(a_rows, axis=0).astype(BF16)

        st = st_ref[hh]
        o = _dot(a, v) + _dot_nt((q * jnp.exp(b)).astype(BF16), st.astype(BF16))
        b_last = edge[nsb]
        k_dec = (k * jnp.exp(b_last - b)).astype(BF16)
        st_ref[hh] = st * jnp.exp(b_last) + _dot_tn(v, k_dec)

        o = o * lax.rsqrt(jnp.mean(o * o, axis=-1, keepdims=True) + EPS) * gain
        r = r_ref[pl.ds(r0, C), vcols].astype(F32)
        o_ref[pl.ds(r0, C), vcols] = (o * (r * _sigmoid(r))).astype(BF16)

    def chunk(c, carry):
        r0 = pl.multiple_of(c * C, C)
        for hh in range(hp):
            one_head(hh, r0)
        return carry

    lax.fori_loop(0, tg // C, chunk, 0, unroll=2)


def _gla(z, zs, wa_t, b_alpha, g_gla, layer, batch, seq, tg=512, hp=4):
    T = z.shape[0]
    nb = seq // tg
    kw, vw = hp * GLA_DK, hp * GLA_DV
    qb, kb, vb, rb = Z_GQ // kw, Z_GK // kw, Z_GV // vw, Z_GR // vw
    return pl.pallas_call(
        functools.partial(_gla_kernel, tg=tg, hp=hp),
        grid=(batch, GLA_HEADS // hp, nb),
        in_specs=[
            pl.BlockSpec((tg, kw), lambda b, h, n: (b * nb + n, qb + h)),
            pl.BlockSpec((tg, kw), lambda b, h, n: (b * nb + n, kb + h)),
            pl.BlockSpec((tg, vw), lambda b, h, n: (b * nb + n, vb + h)),
            pl.BlockSpec((tg, vw), lambda b, h, n: (b * nb + n, rb + h)),
            pl.BlockSpec((tg, LANES), lambda b, h, n: (b * nb + n, ZS_SMALL // LANES)),
            pl.BlockSpec((None, LANES, kw), lambda b, h, n: (layer, 0, h)),
            pl.BlockSpec((None, 1, kw), lambda b, h, n: (layer, 0, h)),
            pl.BlockSpec((None, 1, GLA_DV), lambda b, h, n: (layer, 0, 0)),
        ],
        out_specs=pl.BlockSpec((tg, vw), lambda b, h, n: (b * nb + n, h)),
        out_shape=jax.ShapeDtypeStruct((T, GLA_V_W), BF16),
        scratch_shapes=[pltpu.VMEM((hp, GLA_DV, GLA_DK), F32), pltpu.VMEM((tg, kw), F32)],
        compiler_params=_cparams(("parallel", "parallel", "arbitrary")),
        name="gla",
    )(z, z, z, z, zs, wa_t, b_alpha, g_gla)


def _fox_kernel(qi_ref, ki_ref, q_ref, qx_ref, k_ref, kx_ref, v_ref, o_ref,
                m_ref, l_ref, acc_ref, qa_ref, s_ref, *, tq, tk, strip):
    t = pl.program_id(2)
    qi = qi_ref[t]
    ki = ki_ref[t]
    exp2_scale = (FOX_DH ** -0.5) * float(np.log2(np.e))

    @pl.when(ki == 0)
    def _():
        m_ref[...] = jnp.full_like(m_ref, -jnp.inf)
        l_ref[...] = jnp.zeros_like(l_ref)
        acc_ref[...] = jnp.zeros_like(acc_ref)
        qa_ref[:, 0:FOX_DH] = q_ref[...]
        qa_ref[:, FOX_DH:] = qx_ref[...]

    def step(diag):
        ka = jnp.concatenate([k_ref[...], kx_ref[...]], axis=1)
        v = v_ref[...]
        nstrip = tq // strip

        def width(c):
            return (c + 1) * strip if diag else tk

        def scores(c):
            s_ref[c % 2, :, :width(c)] = _dot_nt(qa_ref[c * strip:(c + 1) * strip, :], ka[:width(c)])

        scores(0)
        for c in range(nstrip):
            rows = slice(c * strip, (c + 1) * strip)
            ncol = width(c)
            if c + 1 < nstrip:
                scores(c + 1)
            s = s_ref[c % 2, :, :ncol]
            if diag:
                row = lax.broadcasted_iota(jnp.int32, (strip, ncol), 0) + c * strip
                col = lax.broadcasted_iota(jnp.int32, (strip, ncol), 1)
                s = jnp.where(col <= row, s, -jnp.inf)
            m_old = m_ref[rows, :]
            m_new = jnp.maximum(m_old, jnp.max(s, axis=-1, keepdims=True))
            p = jnp.exp2((s - m_new) * exp2_scale)
            alpha = jnp.exp2((m_old - m_new) * exp2_scale)
            l_ref[rows, :] = alpha * l_ref[rows, :] + jnp.sum(p, axis=-1, keepdims=True)
            acc_ref[rows, :] = alpha * acc_ref[rows, :] + _dot(p.astype(BF16), v[:ncol])
            m_ref[rows, :] = m_new

    @pl.when(ki < qi)
    def _():
        step(False)

    @pl.when(ki == qi)
    def _():
        step(True)
        o_ref[...] = (acc_ref[...] / l_ref[...]).astype(BF16)


def _fox(z, qx, kx, batch, seq, tq=2048, strip=512):
    T = z.shape[0]
    tk = tq
    nq = seq // tq
    pairs = [(qi, ki) for qi in range(nq) for ki in range(qi + 1)]
    qi_tab = jnp.asarray(np.array([p[0] for p in pairs], np.int32))
    ki_tab = jnp.asarray(np.array([p[1] for p in pairs], np.int32))
    qb, kb, vb = Z_FQ // FOX_DH, Z_FK // FOX_DH, Z_FV // FOX_DH
    grid_spec = pltpu.PrefetchScalarGridSpec(
        num_scalar_prefetch=2,
        grid=(batch, FOX_HEADS, len(pairs)),
        in_specs=[
            pl.BlockSpec((tq, FOX_DH), lambda b, h, t, qt, kt: (b * nq + qt[t], qb + h)),
            pl.BlockSpec((tq, LANES), lambda b, h, t, qt, kt: (b * nq + qt[t], h)),
            pl.BlockSpec((tk, FOX_DH), lambda b, h, t, qt, kt: (b * nq + kt[t], kb + h)),
            pl.BlockSpec((tk, LANES), lambda b, h, t, qt, kt: (b * nq + kt[t], h)),
            pl.BlockSpec((tk, FOX_DH), lambda b, h, t, qt, kt: (b * nq + kt[t], vb + h)),
        ],
        out_specs=pl.BlockSpec((tq, FOX_DH), lambda b, h, t, qt, kt: (b * nq + qt[t], h)),
        scratch_shapes=[
            pltpu.VMEM((tq, 1), F32), pltpu.VMEM((tq, 1), F32),
            pltpu.VMEM((tq, FOX_DH), F32), pltpu.VMEM((tq, FOX_DH + LANES), BF16),
            pltpu.VMEM((2, strip, tk), F32),
        ],
    )
    return pl.pallas_call(
        functools.partial(_fox_kernel, tq=tq, tk=tk, strip=strip),
        grid_spec=grid_spec,
        out_shape=jax.ShapeDtypeStruct((T, FOX_W), BF16),
        compiler_params=_cparams(("parallel", "parallel", "arbitrary")),
        name="fox",
    )(qi_tab, ki_tab, z, qx, z, kx, z)


def _merge_kernel(yp_ref, yg_ref, yf_ref, gl_ref, wp_ref, wg_ref, wf_ref,
                  u0_ref, u1_ref, u2_ref, b0_ref, b1_ref, b2_ref, o_ref):
    gl = gl_ref[...].astype(BF16)
    out = _sigmoid(_dot(gl, u0_ref[...]) + b0_ref[...]) * _dot(yp_ref[...], wp_ref[...])
    out = out + _sigmoid(_dot(gl, u1_ref[...]) + b1_ref[...]) * _dot(yg_ref[...], wg_ref[...])
    out = out + _sigmoid(_dot(gl, u2_ref[...]) + b2_ref[...]) * _dot(yf_ref[...], wf_ref[...])
    o_ref[...] = out.astype(BF16)


def _merge(yp, yg, yf, zs, wpp, wpg, wpf, wgu, b_gate, layer, tm=1024, tn=512):
    T = yp.shape[0]
    nj = D_MODEL // tn
    y_spec = pl.BlockSpec((tm, POOL_WIDTH), lambda i, j: (i, 0))
    w_spec = pl.BlockSpec((None, POOL_WIDTH, tn), lambda i, j: (layer, 0, j))

    def u_spec(br):
        return pl.BlockSpec((None, GATE_RANK, tn), lambda i, j: (layer, 0, br * nj + j))

    def b_spec(br):
        return pl.BlockSpec((None, 1, tn), lambda i, j: (layer, 0, br * nj + j))

    return pl.pallas_call(
        _merge_kernel,
        grid=(T // tm, nj),
        in_specs=[y_spec, y_spec, y_spec,
                  pl.BlockSpec((tm, GATE_RANK), lambda i, j: (i, ZS_GATE // GATE_RANK)),
                  w_spec, w_spec, w_spec,
                  u_spec(0), u_spec(1), u_spec(2), b_spec(0), b_spec(1), b_spec(2)],
        out_specs=pl.BlockSpec((tm, tn), lambda i, j: (i, j)),
        out_shape=jax.ShapeDtypeStruct((T, D_MODEL), BF16),
        compiler_params=_cparams(("parallel", "arbitrary")),
        name="merge",
    )(yp, yg, yf, zs, wpp, wpg, wpf, wgu, wgu, wgu, b_gate, b_gate, b_gate)


def _mm_res_kernel(a_ref, w_ref, res_ref, o_ref):
    o_ref[...] = res_ref[...] + _dot(a_ref[...], w_ref[...])


def _mm_res_norm_kernel(a_ref, w_ref, res_ref, g_ref, o_ref, xg_ref, ssq_ref):
    x = res_ref[...] + _dot(a_ref[...], w_ref[...])
    o_ref[...] = x
    xg_ref[...] = (x * g_ref[...]).astype(BF16)
    _accumulate_ssq(ssq_ref, x, pl.program_id(1))


def _mm_res(a, w, res, layer, tm, tn, name, gain=None, gain_layer=None):
    T, K = a.shape
    N = w.shape[-1]
    in_specs = [
        pl.BlockSpec((tm, K), lambda i, j: (i, 0)),
        pl.BlockSpec((None, K, tn), lambda i, j: (layer, 0, j)),
        pl.BlockSpec((tm, tn), lambda i, j: (i, j)),
    ]
    x_spec = pl.BlockSpec((tm, tn), lambda i, j: (i, j))
    x_shape = jax.ShapeDtypeStruct((T, N), F32)
    if gain is None:
        return pl.pallas_call(
            _mm_res_kernel, grid=(T // tm, N // tn), in_specs=in_specs, out_specs=x_spec,
            out_shape=x_shape, compiler_params=_cparams(("parallel", "arbitrary")), name=name,
        )(a, w, res)
    return pl.pallas_call(
        _mm_res_norm_kernel,
        grid=(T // tm, N // tn),
        in_specs=in_specs + [pl.BlockSpec((None, 1, tn), lambda i, j: (gain_layer, 0, j))],
        out_specs=[x_spec, x_spec, pl.BlockSpec((tm, LANES), lambda i, j: (i, 0))],
        out_shape=[x_shape, jax.ShapeDtypeStruct((T, N), BF16), jax.ShapeDtypeStruct((T, LANES), F32)],
        compiler_params=_cparams(("parallel", "arbitrary")),
        name=name,
    )(a, w, res, gain)


def _ffn_up_kernel(xg_ref, ssq_ref, wg_ref, wu_ref, o_ref, *, tail):
    rinv = _row_rinv(ssq_ref)
    last = pl.num_programs(1) - 1

    def tile(width):
        xg = xg_ref[...]
        gate = _dot(xg, wg_ref[:, :width]) * rinv
        up = _dot(xg, wu_ref[:, :width]) * rinv
        o_ref[:, :width] = (gate * _sigmoid(gate) * up).astype(BF16)

    if tail == o_ref.shape[1]:
        tile(tail)
    else:
        pl.when(pl.program_id(1) != last)(lambda: tile(o_ref.shape[1]))
        pl.when(pl.program_id(1) == last)(lambda: tile(tail))


def _ffn_up(xg, ssq, wg, wu, layer, tm=1024, tn=512):
    T, D = xg.shape
    N = wg.shape[-1]
    w_spec = pl.BlockSpec((None, D, tn), lambda i, j: (layer, 0, j))
    return pl.pallas_call(
        functools.partial(_ffn_up_kernel, tail=N - (pl.cdiv(N, tn) - 1) * tn),
        grid=(T // tm, pl.cdiv(N, tn)),
        in_specs=[
            pl.BlockSpec((tm, D), lambda i, j: (i, 0)),
            pl.BlockSpec((tm, LANES), lambda i, j: (i, 0)),
            w_spec, w_spec,
        ],
        out_specs=pl.BlockSpec((tm, tn), lambda i, j: (i, j)),
        out_shape=jax.ShapeDtypeStruct((T, N), BF16),
        compiler_params=_cparams(("parallel", "arbitrary")),
        name="ffn_up",
    )(xg, ssq, wg, wu)


def _norm_kernel(x_ref, g_ref, o_ref):
    o_ref[...] = _rms_rows(x_ref[...], g_ref[...])


def _final_norm(x, g, tm=512):
    T, D = x.shape
    return pl.pallas_call(
        _norm_kernel,
        grid=(T // tm,),
        in_specs=[pl.BlockSpec((tm, D), lambda i: (i, 0)), pl.BlockSpec((1, D), lambda i: (0, 0))],
        out_specs=pl.BlockSpec((tm, D), lambda i: (i, 0)),
        out_shape=jax.ShapeDtypeStruct((T, D), F32),
        compiler_params=_cparams(("parallel",)),
        name="final_norm",
    )(x, g)


def _split_offsets():
    sizes = (POOL_WIDTH, GLA_QK_W, GLA_QK_W, GLA_V_W, GLA_GATE_RANK, GLA_V_W,
             FOX_W, FOX_W, FOX_W, FOX_HEADS, GATE_RANK)
    offs = np.concatenate([[0], np.cumsum(sizes)])
    return [int(o) for o in offs]


def _prep_w_in(w_in):
    o = _split_offsets()
    a_lo, a_hi = o[4], o[5]
    f_lo, f_hi = o[9], o[10]
    gl_lo, gl_hi = o[10], o[11]
    wb = w_in.astype(BF16)
    main = jnp.concatenate([wb[..., :a_lo], wb[..., a_hi:f_lo]], axis=-1)
    pad = jnp.zeros(wb.shape[:-1] + (ZS_W - GATE_RANK - GLA_GATE_RANK - FOX_HEADS,), BF16)
    small = jnp.concatenate([wb[..., gl_lo:gl_hi], wb[..., a_lo:a_hi], wb[..., f_lo:f_hi], pad], axis=-1)
    return main, small


def kernel(x, g_mix, w_in, w_pool, s_pool, w_alpha, b_alpha, g_gla, b_f, w_gate_up, b_gate,
           w_proj_pool, w_proj_gla, w_proj_fox, w_o, g_ffn, w_ffn_gate, w_ffn_up, w_ffn_down, g_final):
    B, S, D = x.shape
    T = B * S
    L = g_mix.shape[0]
    xf = x.reshape(T, D)

    w_main, w_small = _prep_w_in(w_in)
    w_pool_b = w_pool.astype(BF16)
    wa = jnp.pad(w_alpha, ((0, 0), (ZS_A_LANE, LANES - ZS_A_LANE - GLA_GATE_RANK), (0, 0))).astype(BF16)
    bf_pad = jnp.pad(b_f, ((0, 0), (ZS_F_LANE, LANES - ZS_F_LANE - FOX_HEADS)))[:, None, :]
    wgu = w_gate_up.astype(BF16)
    wpp, wpg, wpf = (w.astype(BF16) for w in (w_proj_pool, w_proj_gla, w_proj_fox))
    wo = w_o.astype(BF16)
    wfg, wfu, wfd = (w.astype(BF16) for w in (w_ffn_gate, w_ffn_up, w_ffn_down))
    g_mix3, g_ffn3 = g_mix[:, None, :], g_ffn[:, None, :]
    s_pool3, b_alpha3, g_gla3, b_gate3 = (a[:, None, :] for a in (s_pool, b_alpha, g_gla, b_gate))

    xg, ssq = _norm_prep(xf, g_mix3, 0)
    for l in range(L):
        z, zs = _inproj(xg, ssq, w_main, w_small, l)
        y_pool = _pool(z, w_pool_b, s_pool3, l, S)
        qx, kx = _fcum(zs, bf_pad, l, S)
        y_gla = _gla(z, zs, wa, b_alpha3, g_gla3, l, B, S)
        y_fox = _fox(z, qx, kx, B, S)
        merged = _merge(y_pool, y_gla, y_fox, zs, wpp, wpg, wpf, wgu, b_gate3, l)
        xf, xg, ssq = _mm_res(merged, wo, xf, l, tm=1024, tn=512, name="out_proj", gain=g_ffn3, gain_layer=l)
        hid = _ffn_up(xg, ssq, wfg, wfu, l)
        if l + 1 < L:
            xf, xg, ssq = _mm_res(hid, wfd, xf, l, tm=512, tn=512, name="ffn_down", gain=g_mix3, gain_layer=l + 1)
        else:
            xf = _mm_res(hid, wfd, xf, l, tm=512, tn=512, name="ffn_down")
    return _final_norm(xf, g_final[None, :]).reshape(B, S, D)
```

```python
import functools

import jax
import jax.numpy as jnp
import numpy as np
from jax import lax
from jax.experimental import pallas as pl
from jax.experimental.pallas import tpu as pltpu

F32 = jnp.float32
BF16 = jnp.bfloat16

D_MODEL = 4096
DEPTH = 4
EPS = 1e-6
POOL_WIDTH = 1024
POOL_WINDOWS = (2, 4, 8, 16)
POOL_GROUP = POOL_WIDTH // len(POOL_WINDOWS)
POOL_HALO = 16
GLA_HEADS = 4
GLA_DK = 128
GLA_DV = 256
GLA_QK_W = GLA_HEADS * GLA_DK
GLA_V_W = GLA_HEADS * GLA_DV
GLA_GATE_RANK = 16
GLA_GATE_NORMALIZER = 16.0
GLA_CHUNK = 64
GLA_SUB = 16
FOX_HEADS = 8
FOX_DH = 128
FOX_W = FOX_HEADS * FOX_DH
N_BRANCH = 3
GATE_RANK = 256
FFN_HIDDEN = ((8 * D_MODEL // 3 + 255) // 256) * 256

LANES = 128
Z_POOL, Z_GQ, Z_GK, Z_GV, Z_GR, Z_FQ, Z_FK, Z_FV = 0, 1024, 1536, 2048, 3072, 4096, 5120, 6144
Z_MAIN_W = 7168
ZS_GATE = 0
ZS_SMALL = GATE_RANK
ZS_A_LANE = 0
ZS_F_LANE = GLA_GATE_RANK
ZS_W = GATE_RANK + LANES

VMEM_LIMIT = 56 * 1024 * 1024


def _cparams(sem):
    return pltpu.CompilerParams(dimension_semantics=sem, vmem_limit_bytes=VMEM_LIMIT)


def _dot(a, b):
    return jnp.dot(a, b, preferred_element_type=F32)


def _dot_nt(a, b):
    return lax.dot_general(a, b, (((1,), (1,)), ((), ())), preferred_element_type=F32)


def _dot_tn(a, b):
    return lax.dot_general(a, b, (((0,), (0,)), ((), ())), preferred_element_type=F32)


def _log_sigmoid(x):
    return jnp.minimum(x, 0.0) - jnp.log(1.0 + jnp.exp(-jnp.abs(x)))


def _sigmoid(x):
    return 1.0 / (1.0 + jnp.exp(-x))


def _tril_ones(n):
    row = lax.broadcasted_iota(jnp.int32, (n, n), 0)
    col = lax.broadcasted_iota(jnp.int32, (n, n), 1)
    return (row >= col).astype(BF16)


def _cumsum_rows(tril, x):
    hi = x.astype(BF16)
    rest = x - hi.astype(F32)
    mid = rest.astype(BF16)
    lo = (rest - mid.astype(F32)).astype(BF16)
    return _dot(tril, hi) + _dot(tril, mid) + _dot(tril, lo)


def _rms_rows(x, g):
    return x * lax.rsqrt(jnp.mean(x * x, axis=-1, keepdims=True) + EPS) * g


def _row_rinv(ssq_ref):
    return lax.rsqrt(ssq_ref[:, 0:1] * (1.0 / D_MODEL) + EPS)


def _accumulate_ssq(ssq_ref, row_sums, j):
    part = jnp.broadcast_to(row_sums, ssq_ref.shape)

    @pl.when(j == 0)
    def _():
        ssq_ref[...] = part

    @pl.when(j != 0)
    def _():
        ssq_ref[...] += part


def _norm_prep_kernel(x_ref, g_ref, xg_ref, ssq_ref):
    x = x_ref[...]
    xg_ref[...] = (x * g_ref[...]).astype(BF16)
    ssq_ref[...] = jnp.broadcast_to(jnp.sum(x * x, axis=-1, keepdims=True), ssq_ref.shape)


def _norm_prep(x, g, layer, tm=512):
    T, D = x.shape
    return pl.pallas_call(
        _norm_prep_kernel,
        grid=(T // tm,),
        in_specs=[pl.BlockSpec((tm, D), lambda i: (i, 0)),
                  pl.BlockSpec((None, 1, D), lambda i: (layer, 0, 0))],
        out_specs=[pl.BlockSpec((tm, D), lambda i: (i, 0)),
                   pl.BlockSpec((tm, LANES), lambda i: (i, 0))],
        out_shape=[jax.ShapeDtypeStruct((T, D), BF16), jax.ShapeDtypeStruct((T, LANES), F32)],
        compiler_params=_cparams(("parallel",)),
        name="norm_prep",
    )(x, g)


def _inproj_kernel(xg_ref, ssq_ref, w_ref, ws_ref, z_ref, zs_ref):
    rinv = _row_rinv(ssq_ref)

    @pl.when(pl.program_id(1) == 0)
    def _():
        zs_ref[...] = _dot(xg_ref[...], ws_ref[...]) * rinv

    z_ref[...] = (_dot(xg_ref[...], w_ref[...]) * rinv).astype(BF16)


def _inproj(xg, ssq, w_main, w_small, layer, tm=1024, tn=1024):
    T, D = xg.shape
    n_main = w_main.shape[-1]
    n_small = w_small.shape[-1]
    return pl.pallas_call(
        _inproj_kernel,
        grid=(T // tm, n_main // tn),
        in_specs=[
            pl.BlockSpec((tm, D), lambda i, j: (i, 0)),
            pl.BlockSpec((tm, LANES), lambda i, j: (i, 0)),
            pl.BlockSpec((None, D, tn), lambda i, j: (layer, 0, j)),
            pl.BlockSpec((None, D, n_small), lambda i, j: (layer, 0, 0), pipeline_mode=pl.Buffered(1)),
        ],
        out_specs=[
            pl.BlockSpec((tm, tn), lambda i, j: (i, j)),
            pl.BlockSpec((tm, n_small), lambda i, j: (i, 0)),
        ],
        out_shape=[
            jax.ShapeDtypeStruct((T, n_main), BF16),
            jax.ShapeDtypeStruct((T, n_small), F32),
        ],
        compiler_params=_cparams(("parallel", "arbitrary")),
        name="inproj",
    )(xg, ssq, w_main, w_small)


def _pool_kernel(cur_ref, prev_ref, w_ref, s_ref, o_ref, ext_ref, *, tm, blocks_per_seq):
    blk = pl.program_id(0) % blocks_per_seq
    cur = cur_ref[...].astype(F32)
    ext_ref[0:POOL_HALO, :] = jnp.where(blk == 0, 0.0, prev_ref[...].astype(F32))
    ext_ref[POOL_HALO:, :] = cur
    pos = blk * tm + lax.broadcasted_iota(jnp.int32, (tm, 1), 0)
    for gi, win in enumerate(POOL_WINDOWS):
        lo = gi * POOL_GROUP
        u = cur[:, lo:lo + POOL_GROUP]
        acc = u
        for s in range(1, win):
            acc = acc + ext_ref[POOL_HALO - s:POOL_HALO - s + tm, lo:lo + POOL_GROUP]
        count = jnp.minimum(pos + 1, win).astype(F32)
        p = (acc / count - u).astype(BF16)
        y = _dot(p, w_ref[gi]) * s_ref[:, lo:lo + POOL_GROUP]
        o_ref[:, lo:lo + POOL_GROUP] = y.astype(BF16)


def _pool(z, w_pool, s_pool, layer, seq, tm=512):
    T = z.shape[0]
    halo_per_tile = tm // POOL_HALO
    return pl.pallas_call(
        functools.partial(_pool_kernel, tm=tm, blocks_per_seq=seq // tm),
        grid=(T // tm,),
        in_specs=[
            pl.BlockSpec((tm, POOL_WIDTH), lambda i: (i, Z_POOL // POOL_WIDTH)),
            pl.BlockSpec((POOL_HALO, POOL_WIDTH),
                         lambda i: (jnp.maximum(i * halo_per_tile - 1, 0), Z_POOL // POOL_WIDTH)),
            pl.BlockSpec((None, len(POOL_WINDOWS), POOL_GROUP, POOL_GROUP), lambda i: (layer, 0, 0, 0)),
            pl.BlockSpec((None, 1, POOL_WIDTH), lambda i: (layer, 0, 0)),
        ],
        out_specs=pl.BlockSpec((tm, POOL_WIDTH), lambda i: (i, 0)),
        out_shape=jax.ShapeDtypeStruct((T, POOL_WIDTH), BF16),
        scratch_shapes=[pltpu.VMEM((tm + POOL_HALO, POOL_WIDTH), F32)],
        compiler_params=_cparams(("parallel",)),
        name="pool",
    )(z, z, w_pool, s_pool)


def _fcum_kernel(zs_ref, bf_ref, qx_ref, kx_ref, carry_ref, *, tc, blocks_per_seq):
    @pl.when(pl.program_id(0) % blocks_per_seq == 0)
    def _():
        carry_ref[...] = jnp.zeros_like(carry_ref)

    lf = _log_sigmoid(zs_ref[...] + bf_ref[...])
    c = _cumsum_rows(_tril_ones(tc), lf) + carry_ref[...]
    carry_ref[...] = c[tc - 1:tc, :]

    lane = lax.broadcasted_iota(jnp.int32, (tc, LANES), 1)
    ones_q = ((lane >= 3) & (lane < 6)).astype(F32)
    for h in range(FOX_HEADS):
        big = jnp.broadcast_to(c[:, ZS_F_LANE + h:ZS_F_LANE + h + 1], (tc, LANES)) * (FOX_DH ** 0.5)
        hi = big.astype(BF16).astype(F32)
        rest = big - hi
        mid = rest.astype(BF16).astype(F32)
        lo = rest - mid
        trip = jnp.where(lane % 3 == 0, hi, jnp.where(lane % 3 == 1, mid, lo))
        qx_ref[:, h * LANES:(h + 1) * LANES] = jnp.where(lane < 3, trip, ones_q).astype(BF16)
        kx_ref[:, h * LANES:(h + 1) * LANES] = jnp.where(
            lane < 3, 1.0, jnp.where(lane < 6, -trip, 0.0)).astype(BF16)


def _fcum(zs, bf_pad, layer, seq, tc=512):
    T = zs.shape[0]
    x_spec = pl.BlockSpec((tc, FOX_HEADS * LANES), lambda i: (i, 0))
    x_shape = jax.ShapeDtypeStruct((T, FOX_HEADS * LANES), BF16)
    return pl.pallas_call(
        functools.partial(_fcum_kernel, tc=tc, blocks_per_seq=seq // tc),
        grid=(T // tc,),
        in_specs=[
            pl.BlockSpec((tc, LANES), lambda i: (i, ZS_SMALL // LANES)),
            pl.BlockSpec((None, 1, LANES), lambda i: (layer, 0, 0)),
        ],
        out_specs=[x_spec, x_spec],
        out_shape=[x_shape, x_shape],
        scratch_shapes=[pltpu.VMEM((1, LANES), F32)],
        compiler_params=_cparams(("arbitrary",)),
        name="fcum",
    )(zs, bf_pad)


def _gla_kernel(q_ref, k_ref, v_ref, r_ref, a_ref, wa_ref, ba_ref, gg_ref, o_ref,
                st_ref, g_ref, *, tg, hp):
    @pl.when(pl.program_id(2) == 0)
    def _():
        st_ref[...] = jnp.zeros_like(st_ref)

    logits = _dot(a_ref[...].astype(BF16), wa_ref[...]) + ba_ref[...]
    g_ref[...] = _log_sigmoid(logits) * (1.0 / GLA_GATE_NORMALIZER)

    C, SB = GLA_CHUNK, GLA_SUB
    nsb = C // SB
    row = lax.broadcasted_iota(jnp.int32, (C, C), 0)
    col = lax.broadcasted_iota(jnp.int32, (C, C), 1)
    tril = _tril_ones(C)
    sdiff = row // SB - col // SB
    lane_sb = lax.broadcasted_iota(jnp.int32, (SB, C), 1)
    row_sb = lax.broadcasted_iota(jnp.int32, (SB, C), 0)
    gain = gg_ref[...]

    def rows4(parts):
        return jnp.concatenate([jnp.broadcast_to(p, (SB, GLA_DK)) for p in parts], axis=0)

    def one_head(hh, r0):
        kcols = slice(hh * GLA_DK, (hh + 1) * GLA_DK)
        vcols = slice(hh * GLA_DV, (hh + 1) * GLA_DV)
        q = q_ref[pl.ds(r0, C), kcols].astype(F32) * (GLA_DK ** -0.5)
        k = k_ref[pl.ds(r0, C), kcols].astype(F32)
        v = v_ref[pl.ds(r0, C), vcols]
        g = g_ref[pl.ds(r0, C), kcols]
        b = _cumsum_rows(tril, g)
        edge = [jnp.zeros((1, GLA_DK), F32)] + [b[SB * (i + 1) - 1:SB * (i + 1), :] for i in range(nsb)]
        zero = edge[0]
        rb = rows4(edge[0:nsb])
        rn = rows4(edge[1:nsb + 1])
        qt = q * jnp.exp(b - rb)
        kt = (k * jnp.exp(rn - b)).astype(BF16)
        a_off = jnp.where(sdiff == 1, _dot_nt(qt.astype(BF16), kt), 0.0)
        for s in range(2, nsb):
            d = rows4([zero] * s + [jnp.exp(edge[i] - edge[i - s + 1]) for i in range(s, nsb)])
            a_off = a_off + jnp.where(sdiff == s, _dot_nt((qt * d).astype(BF16), kt), 0.0)
        a_rows = []
        b2 = b * float(np.log2(np.e))
        for sb in range(nsb):
            sl = slice(SB * sb, SB * (sb + 1))
            q_s, k_s, b_s = q[sl], k[sl], b2[sl]
            acc = a_off[sl]
            for j in range(SB):
                e = jnp.exp2(b_s - b_s[j:j + 1])
                colv = jnp.sum(q_s * (k_s[j:j + 1] * e), axis=-1, keepdims=True)
                acc = jnp.where((lane_sb == SB * sb + j) & (row_sb >= j), colv, acc)
            a_rows.append(acc)
        a = jnp.concatenate(a_rows, axis=0).astype(BF16)

        st = st_ref[hh]
        o = _dot(a, v) + _dot_nt((q * jnp.exp(b)).astype(BF16), st.astype(BF16))
        b_last = edge[nsb]
        k_dec = (k * jnp.exp(b_last - b)).astype(BF16)
        st_ref[hh] = st * jnp.exp(b_last) + _dot_tn(v, k_dec)

        o = o * lax.rsqrt(jnp.mean(o * o, axis=-1, keepdims=True) + EPS) * gain
        r = r_ref[pl.ds(r0, C), vcols].astype(F32)
        o_ref[pl.ds(r0, C), vcols] = (o * (r * _sigmoid(r))).astype(BF16)

    def chunk(c, carry):
        r0 = pl.multiple_of(c * C, C)
        for hh in range(hp):
            one_head(hh, r0)
        return carry

    lax.fori_loop(0, tg // C, chunk, 0, unroll=2)


def _gla(z, zs, wa_t, b_alpha, g_gla, layer, batch, seq, tg=512, hp=4):
    T = z.shape[0]
    nb = seq // tg
    kw, vw = hp * GLA_DK, hp * GLA_DV
    qb, kb, vb, rb = Z_GQ // kw, Z_GK // kw, Z_GV // vw, Z_GR // vw
    return pl.pallas_call(
        functools.partial(_gla_kernel, tg=tg, hp=hp),
        grid=(batch, GLA_HEADS // hp, nb),
        in_specs=[
            pl.BlockSpec((tg, kw), lambda b, h, n: (b * nb + n, qb + h)),
            pl.BlockSpec((tg, kw), lambda b, h, n: (b * nb + n, kb + h)),
            pl.BlockSpec((tg, vw), lambda b, h, n: (b * nb + n, vb + h)),
            pl.BlockSpec((tg, vw), lambda b, h, n: (b * nb + n, rb + h)),
            pl.BlockSpec((tg, LANES), lambda b, h, n: (b * nb + n, ZS_SMALL // LANES)),
            pl.BlockSpec((None, LANES, kw), lambda b, h, n: (layer, 0, h)),
            pl.BlockSpec((None, 1, kw), lambda b, h, n: (layer, 0, h)),
            pl.BlockSpec((None, 1, GLA_DV), lambda b, h, n: (layer, 0, 0)),
        ],
        out_specs=pl.BlockSpec((tg, vw), lambda b, h, n: (b * nb + n, h)),
        out_shape=jax.ShapeDtypeStruct((T, GLA_V_W), BF16),
        scratch_shapes=[pltpu.VMEM((hp, GLA_DV, GLA_DK), F32), pltpu.VMEM((tg, kw), F32)],
        compiler_params=_cparams(("parallel", "parallel", "arbitrary")),
        name="gla",
    )(z, z, z, z, zs, wa_t, b_alpha, g_gla)


def _fox_kernel(qi_ref, ki_ref, q_ref, qx_ref, k_ref, kx_ref, v_ref, o_ref,
                m_ref, l_ref, acc_ref, qa_ref, s_ref, *, tq, tk, strip):
    t = pl.program_id(2)
    qi = qi_ref[t]
    ki = ki_ref[t]
    exp2_scale = (FOX_DH ** -0.5) * float(np.log2(np.e))

    @pl.when(ki == 0)
    def _():
        m_ref[...] = jnp.full_like(m_ref, -jnp.inf)
        l_ref[...] = jnp.zeros_like(l_ref)
        acc_ref[...] = jnp.zeros_like(acc_ref)
        qa_ref[:, 0:FOX_DH] = q_ref[...]
        qa_ref[:, FOX_DH:] = qx_ref[...]

    def step(diag):
        ka = jnp.concatenate([k_ref[...], kx_ref[...]], axis=1)
        v = v_ref[...]
        nstrip = tq // strip

        def width(c):
            return (c + 1) * strip if diag else tk

        def scores(c):
            s_ref[c % 2, :, :width(c)] = _dot_nt(qa_ref[c * strip:(c + 1) * strip, :], ka[:width(c)])

        scores(0)
        for c in range(nstrip):
            rows = slice(c * strip, (c + 1) * strip)
            ncol = width(c)
            if c + 1 < nstrip:
                scores(c + 1)
            s = s_ref[c % 2, :, :ncol]
            if diag:
                row = lax.broadcasted_iota(jnp.int32, (strip, ncol), 0) + c * strip
                col = lax.broadcasted_iota(jnp.int32, (strip, ncol), 1)
                s = jnp.where(col <= row, s, -jnp.inf)
            m_old = m_ref[rows, :]
            m_new = jnp.maximum(m_old, jnp.max(s, axis=-1, keepdims=True))
            p = jnp.exp2((s - m_new) * exp2_scale)
            alpha = jnp.exp2((m_old - m_new) * exp2_scale)
            l_ref[rows, :] = alpha * l_ref[rows, :] + jnp.sum(p, axis=-1, keepdims=True)
            acc_ref[rows, :] = alpha * acc_ref[rows, :] + _dot(p.astype(BF16), v[:ncol])
            m_ref[rows, :] = m_new

    @pl.when(ki < qi)
    def _():
        step(False)

    @pl.when(ki == qi)
    def _():
        step(True)
        o_ref[...] = (acc_ref[...] / l_ref[...]).astype(BF16)


def _fox(z, qx, kx, batch, seq, tq=2048, strip=512):
    T = z.shape[0]
    tk = tq
    nq = seq // tq
    pairs = [(qi, ki) for qi in range(nq) for ki in range(qi + 1)]
    qi_tab = jnp.asarray(np.array([p[0] for p in pairs], np.int32))
    ki_tab = jnp.asarray(np.array([p[1] for p in pairs], np.int32))
    qb, kb, vb = Z_FQ // FOX_DH, Z_FK // FOX_DH, Z_FV // FOX_DH
    grid_spec = pltpu.PrefetchScalarGridSpec(
        num_scalar_prefetch=2,
        grid=(batch, FOX_HEADS, len(pairs)),
        in_specs=[
            pl.BlockSpec((tq, FOX_DH), lambda b, h, t, qt, kt: (b * nq + qt[t], qb + h)),
            pl.BlockSpec((tq, LANES), lambda b, h, t, qt, kt: (b * nq + qt[t], h)),
            pl.BlockSpec((tk, FOX_DH), lambda b, h, t, qt, kt: (b * nq + kt[t], kb + h)),
            pl.BlockSpec((tk, LANES), lambda b, h, t, qt, kt: (b * nq + kt[t], h)),
            pl.BlockSpec((tk, FOX_DH), lambda b, h, t, qt, kt: (b * nq + kt[t], vb + h)),
        ],
        out_specs=pl.BlockSpec((tq, FOX_DH), lambda b, h, t, qt, kt: (b * nq + qt[t], h)),
        scratch_shapes=[
            pltpu.VMEM((tq, 1), F32), pltpu.VMEM((tq, 1), F32),
            pltpu.VMEM((tq, FOX_DH), F32), pltpu.VMEM((tq, FOX_DH + LANES), BF16),
            pltpu.VMEM((2, strip, tk), F32),
        ],
    )
    return pl.pallas_call(
        functools.partial(_fox_kernel, tq=tq, tk=tk, strip=strip),
        grid_spec=grid_spec,
        out_shape=jax.ShapeDtypeStruct((T, FOX_W), BF16),
        compiler_params=_cparams(("parallel", "parallel", "arbitrary")),
        name="fox",
    )(qi_tab, ki_tab, z, qx, z, kx, z)


def _merge_kernel(yp_ref, yg_ref, yf_ref, gl_ref, wp_ref, wg_ref, wf_ref,
                  u0_ref, u1_ref, u2_ref, b0_ref, b1_ref, b2_ref, o_ref):
    gl = gl_ref[...].astype(BF16)
    out = _sigmoid(_dot(gl, u0_ref[...]) + b0_ref[...]) * _dot(yp_ref[...], wp_ref[...])
    out = out + _sigmoid(_dot(gl, u1_ref[...]) + b1_ref[...]) * _dot(yg_ref[...], wg_ref[...])
    out = out + _sigmoid(_dot(gl, u2_ref[...]) + b2_ref[...]) * _dot(yf_ref[...], wf_ref[...])
    o_ref[...] = out.astype(BF16)


def _merge(yp, yg, yf, zs, wpp, wpg, wpf, wgu, b_gate, layer, tm=1024, tn=512):
    T = yp.shape[0]
    nj = D_MODEL // tn
    y_spec = pl.BlockSpec((tm, POOL_WIDTH), lambda i, j: (i, 0))
    w_spec = pl.BlockSpec((None, POOL_WIDTH, tn), lambda i, j: (layer, 0, j))

    def u_spec(br):
        return pl.BlockSpec((None, GATE_RANK, tn), lambda i, j: (layer, 0, br * nj + j))

    def b_spec(br):
        return pl.BlockSpec((None, 1, tn), lambda i, j: (layer, 0, br * nj + j))

    return pl.pallas_call(
        _merge_kernel,
        grid=(T // tm, nj),
        in_specs=[y_spec, y_spec, y_spec,
                  pl.BlockSpec((tm, GATE_RANK), lambda i, j: (i, ZS_GATE // GATE_RANK)),
                  w_spec, w_spec, w_spec,
                  u_spec(0), u_spec(1), u_spec(2), b_spec(0), b_spec(1), b_spec(2)],
        out_specs=pl.BlockSpec((tm, tn), lambda i, j: (i, j)),
        out_shape=jax.ShapeDtypeStruct((T, D_MODEL), BF16),
        compiler_params=_cparams(("parallel", "arbitrary")),
        name="merge",
    )(yp, yg, yf, zs, wpp, wpg, wpf, wgu, wgu, wgu, b_gate, b_gate, b_gate)


def _mm_res_kernel(a_ref, w_ref, res_ref, o_ref):
    o_ref[...] = res_ref[...] + _dot(a_ref[...], w_ref[...])


def _mm_res_norm_kernel(a_ref, w_ref, res_ref, g_ref, o_ref, xg_ref, ssq_ref):
    x = res_ref[...] + _dot(a_ref[...], w_ref[...])
    o_ref[...] = x
    xg_ref[...] = (x * g_ref[...]).astype(BF16)
    _accumulate_ssq(ssq_ref, jnp.sum(x * x, axis=-1, keepdims=True), pl.program_id(1))


def _mm_res(a, w, res, layer, tm, tn, name, gain=None, gain_layer=None):
    T, K = a.shape
    N = w.shape[-1]
    in_specs = [
        pl.BlockSpec((tm, K), lambda i, j: (i, 0)),
        pl.BlockSpec((None, K, tn), lambda i, j: (layer, 0, j)),
        pl.BlockSpec((tm, tn), lambda i, j: (i, j)),
    ]
    x_spec = pl.BlockSpec((tm, tn), lambda i, j: (i, j))
    x_shape = jax.ShapeDtypeStruct((T, N), F32)
    if gain is None:
        return pl.pallas_call(
            _mm_res_kernel, grid=(T // tm, N // tn), in_specs=in_specs, out_specs=x_spec,
            out_shape=x_shape, compiler_params=_cparams(("parallel", "arbitrary")), name=name,
        )(a, w, res)
    return pl.pallas_call(
        _mm_res_norm_kernel,
        grid=(T // tm, N // tn),
        in_specs=in_specs + [pl.BlockSpec((None, 1, tn), lambda i, j: (gain_layer, 0, j))],
        out_specs=[x_spec, x_spec, pl.BlockSpec((tm, LANES), lambda i, j: (i, 0))],
        out_shape=[x_shape, jax.ShapeDtypeStruct((T, N), BF16), jax.ShapeDtypeStruct((T, LANES), F32)],
        compiler_params=_cparams(("parallel", "arbitrary")),
        name=name,
    )(a, w, res, gain)


def _ffn_up_kernel(xg_ref, ssq_ref, wg_ref, wu_ref, o_ref, *, tail):
    rinv = _row_rinv(ssq_ref)
    last = pl.num_programs(1) - 1

    def tile(width):
        xg = xg_ref[...]
        gate = _dot(xg, wg_ref[:, :width]) * rinv
        up = _dot(xg, wu_ref[:, :width]) * rinv
        o_ref[:, :width] = (gate * _sigmoid(gate) * up).astype(BF16)

    if tail == o_ref.shape[1]:
        tile(tail)
    else:
        pl.when(pl.program_id(1) != last)(lambda: tile(o_ref.shape[1]))
        pl.when(pl.program_id(1) == last)(lambda: tile(tail))


def _ffn_up(xg, ssq, wg, wu, layer, tm=1024, tn=512):
    T, D = xg.shape
    N = wg.shape[-1]
    w_spec = pl.BlockSpec((None, D, tn), lambda i, j: (layer, 0, j))
    return pl.pallas_call(
        functools.partial(_ffn_up_kernel, tail=N - (pl.cdiv(N, tn) - 1) * tn),
        grid=(T // tm, pl.cdiv(N, tn)),
        in_specs=[
            pl.BlockSpec((tm, D), lambda i, j: (i, 0)),
            pl.BlockSpec((tm, LANES), lambda i, j: (i, 0)),
            w_spec, w_spec,
        ],
        out_specs=pl.BlockSpec((tm, tn), lambda i, j: (i, j)),
        out_shape=jax.ShapeDtypeStruct((T, N), BF16),
        compiler_params=_cparams(("parallel", "arbitrary")),
        name="ffn_up",
    )(xg, ssq, wg, wu)


def _norm_kernel(x_ref, g_ref, o_ref):
    o_ref[...] = _rms_rows(x_ref[...], g_ref[...])


def _final_norm(x, g, tm=512):
    T, D = x.shape
    return pl.pallas_call(
        _norm_kernel,
        grid=(T // tm,),
        in_specs=[pl.BlockSpec((tm, D), lambda i: (i, 0)), pl.BlockSpec((1, D), lambda i: (0, 0))],
        out_specs=pl.BlockSpec((tm, D), lambda i: (i, 0)),
        out_shape=jax.ShapeDtypeStruct((T, D), F32),
        compiler_params=_cparams(("parallel",)),
        name="final_norm",
    )(x, g)


def _split_offsets():
    sizes = (POOL_WIDTH, GLA_QK_W, GLA_QK_W, GLA_V_W, GLA_GATE_RANK, GLA_V_W,
             FOX_W, FOX_W, FOX_W, FOX_HEADS, GATE_RANK)
    offs = np.concatenate([[0], np.cumsum(sizes)])
    return [int(o) for o in offs]


def _regroup_kernel(a_ref, b_ref, o_ref, *, first_shifted, shift):
    jt = pl.program_id(2)
    tn = o_ref.shape[1]

    @pl.when(jt < first_shifted)
    def _():
        o_ref[...] = a_ref[...].astype(BF16)

    @pl.when(jt >= first_shifted)
    def _():
        x = jnp.concatenate([a_ref[...], b_ref[...]], axis=1)
        o_ref[...] = x[:, shift:shift + tn].astype(BF16)


def _prep_w_in(w_in):
    o = _split_offsets()
    a_lo, a_hi = o[4], o[5]
    f_lo, f_hi = o[9], o[10]
    gl_lo, gl_hi = o[10], o[11]
    L, D, _ = w_in.shape
    tr, tn = 1024, 512
    assert a_lo % tn == 0 and Z_MAIN_W % tn == 0 and f_lo - a_hi == Z_MAIN_W - a_lo and a_hi - a_lo < LANES
    main = pl.pallas_call(
        functools.partial(_regroup_kernel, first_shifted=a_lo // tn, shift=a_hi - a_lo),
        grid=(L, D // tr, Z_MAIN_W // tn),
        in_specs=[
            pl.BlockSpec((None, tr, tn), lambda l, r, j: (l, r, j)),
            pl.BlockSpec((None, tr, LANES), lambda l, r, j: (l, r, (j + 1) * (tn // LANES))),
        ],
        out_specs=pl.BlockSpec((None, tr, tn), lambda l, r, j: (l, r, j)),
        out_shape=jax.ShapeDtypeStruct((L, D, Z_MAIN_W), BF16),
        compiler_params=_cparams(("parallel", "parallel", "parallel")),
        name="w_in_regroup",
    )(w_in, w_in)
    pad = jnp.zeros(w_in.shape[:-1] + (ZS_W - GATE_RANK - GLA_GATE_RANK - FOX_HEADS,), w_in.dtype)
    small = jnp.concatenate([w_in[..., gl_lo:gl_hi], w_in[..., a_lo:a_hi], w_in[..., f_lo:f_hi], pad],
                            axis=-1).astype(BF16)
    return main, small


def kernel(x, g_mix, w_in, w_pool, s_pool, w_alpha, b_alpha, g_gla, b_f, w_gate_up, b_gate,
           w_proj_pool, w_proj_gla, w_proj_fox, w_o, g_ffn, w_ffn_gate, w_ffn_up, w_ffn_down, g_final):
    B, S, D = x.shape
    T = B * S
    L = g_mix.shape[0]
    xf = x.reshape(T, D)

    w_main, w_small = _prep_w_in(w_in)
    w_pool_b = w_pool.astype(BF16)
    wa = jnp.pad(w_alpha, ((0, 0), (ZS_A_LANE, LANES - ZS_A_LANE - GLA_GATE_RANK), (0, 0))).astype(BF16)
    bf_pad = jnp.pad(b_f, ((0, 0), (ZS_F_LANE, LANES - ZS_F_LANE - FOX_HEADS)))[:, None, :]
    wgu = w_gate_up.astype(BF16)
    wpp, wpg, wpf = (w.astype(BF16) for w in (w_proj_pool, w_proj_gla, w_proj_fox))
    wo = w_o.astype(BF16)
    wfg, wfu, wfd = (w.astype(BF16) for w in (w_ffn_gate, w_ffn_up, w_ffn_down))
    g_mix3, g_ffn3 = g_mix[:, None, :], g_ffn[:, None, :]
    s_pool3, b_alpha3, g_gla3, b_gate3 = (a[:, None, :] for a in (s_pool, b_alpha, g_gla, b_gate))

    xg, ssq = _norm_prep(xf, g_mix3, 0)
    for l in range(L):
        z, zs = _inproj(xg, ssq, w_main, w_small, l)
        y_pool = _pool(z, w_pool_b, s_pool3, l, S)
        qx, kx = _fcum(zs, bf_pad, l, S)
        y_gla = _gla(z, zs, wa, b_alpha3, g_gla3, l, B, S)
        y_fox = _fox(z, qx, kx, B, S)
        merged = _merge(y_pool, y_gla, y_fox, zs, wpp, wpg, wpf, wgu, b_gate3, l)
        xf, xg, ssq = _mm_res(merged, wo, xf, l, tm=1024, tn=512, name="out_proj", gain=g_ffn3, gain_layer=l)
        hid = _ffn_up(xg, ssq, wfg, wfu, l)
        if l + 1 < L:
            xf, xg, ssq = _mm_res(hid, wfd, xf, l, tm=512, tn=512, name="ffn_down", gain=g_mix3, gain_layer=l + 1)
        else:
            xf = _mm_res(hid, wfd, xf, l, tm=512, tn=512, name="ffn_down")
    return _final_norm(xf, g_final[None, :]).reshape(B, S, D)
```

```python
import functools

import jax
import jax.numpy as jnp
import numpy as np
from jax import lax
from jax.experimental import pallas as pl
from jax.experimental.pallas import tpu as pltpu

F32 = jnp.float32
BF16 = jnp.bfloat16

D_MODEL = 4096
DEPTH = 4
EPS = 1e-6
POOL_WIDTH = 1024
POOL_WINDOWS = (2, 4, 8, 16)
POOL_GROUP = POOL_WIDTH // len(POOL_WINDOWS)
POOL_HALO = 16
GLA_HEADS = 4
GLA_DK = 128
GLA_DV = 256
GLA_QK_W = GLA_HEADS * GLA_DK
GLA_V_W = GLA_HEADS * GLA_DV
GLA_GATE_RANK = 16
GLA_GATE_NORMALIZER = 16.0
GLA_CHUNK = 64
GLA_SUB = 16
FOX_HEADS = 8
FOX_DH = 128
FOX_W = FOX_HEADS * FOX_DH
N_BRANCH = 3
GATE_RANK = 256
FFN_HIDDEN = ((8 * D_MODEL // 3 + 255) // 256) * 256

LANES = 128
Z_POOL, Z_GQ, Z_GK, Z_GV, Z_GR, Z_FQ, Z_FK, Z_FV = 0, 1024, 1536, 2048, 3072, 4096, 5120, 6144
Z_MAIN_W = 7168
ZS_GATE = 0
ZS_SMALL = GATE_RANK
ZS_A_LANE = 0
ZS_F_LANE = GLA_GATE_RANK
ZS_W = GATE_RANK + LANES

VMEM_LIMIT = 56 * 1024 * 1024


def _cparams(sem):
    return pltpu.CompilerParams(dimension_semantics=sem, vmem_limit_bytes=VMEM_LIMIT)


def _dot(a, b):
    return jnp.dot(a, b, preferred_element_type=F32)


def _dot_nt(a, b):
    return lax.dot_general(a, b, (((1,), (1,)), ((), ())), preferred_element_type=F32)


def _dot_tn(a, b):
    return lax.dot_general(a, b, (((0,), (0,)), ((), ())), preferred_element_type=F32)


def _log_sigmoid(x):
    return jnp.minimum(x, 0.0) - jnp.log(1.0 + jnp.exp(-jnp.abs(x)))


def _sigmoid(x):
    return 1.0 / (1.0 + jnp.exp(-x))


def _tril_ones(n):
    row = lax.broadcasted_iota(jnp.int32, (n, n), 0)
    col = lax.broadcasted_iota(jnp.int32, (n, n), 1)
    return (row >= col).astype(BF16)


def _cumsum_rows(tril, x):
    hi = x.astype(BF16)
    rest = x - hi.astype(F32)
    mid = rest.astype(BF16)
    lo = (rest - mid.astype(F32)).astype(BF16)
    return _dot(tril, hi) + _dot(tril, mid) + _dot(tril, lo)


def _rms_rows(x, g):
    return x * lax.rsqrt(jnp.mean(x * x, axis=-1, keepdims=True) + EPS) * g


def _row_rinv(ssq_ref):
    return lax.rsqrt(ssq_ref[:, 0:1] * (1.0 / D_MODEL) + EPS)


def _accumulate_ssq(ssq_ref, row_sums, j):
    part = jnp.broadcast_to(row_sums, ssq_ref.shape)

    @pl.when(j == 0)
    def _():
        ssq_ref[...] = part

    @pl.when(j != 0)
    def _():
        ssq_ref[...] += part


def _norm_prep_kernel(x_ref, g_ref, xg_ref, ssq_ref):
    x = x_ref[...]
    xg_ref[...] = (x * g_ref[...]).astype(BF16)
    ssq_ref[...] = jnp.broadcast_to(jnp.sum(x * x, axis=-1, keepdims=True), ssq_ref.shape)


def _norm_prep(x, g, layer, tm=512):
    T, D = x.shape
    return pl.pallas_call(
        _norm_prep_kernel,
        grid=(T // tm,),
        in_specs=[pl.BlockSpec((tm, D), lambda i: (i, 0)),
                  pl.BlockSpec((None, 1, D), lambda i: (layer, 0, 0))],
        out_specs=[pl.BlockSpec((tm, D), lambda i: (i, 0)),
                   pl.BlockSpec((tm, LANES), lambda i: (i, 0))],
        out_shape=[jax.ShapeDtypeStruct((T, D), BF16), jax.ShapeDtypeStruct((T, LANES), F32)],
        compiler_params=_cparams(("parallel",)),
        name="norm_prep",
    )(x, g)


def _inproj_kernel(xg_ref, ssq_ref, w_ref, ws_ref, z_ref, zs_ref):
    rinv = _row_rinv(ssq_ref)

    @pl.when(pl.program_id(1) == 0)
    def _():
        zs_ref[...] = _dot(xg_ref[...], ws_ref[...]) * rinv

    z_ref[...] = (_dot(xg_ref[...], w_ref[...]) * rinv).astype(BF16)


def _inproj(xg, ssq, w_main, w_small, layer, tm=1024, tn=1024):
    T, D = xg.shape
    n_main = w_main.shape[-1]
    n_small = w_small.shape[-1]
    return pl.pallas_call(
        _inproj_kernel,
        grid=(T // tm, n_main // tn),
        in_specs=[
            pl.BlockSpec((tm, D), lambda i, j: (i, 0)),
            pl.BlockSpec((tm, LANES), lambda i, j: (i, 0)),
            pl.BlockSpec((None, D, tn), lambda i, j: (layer, 0, j)),
            pl.BlockSpec((None, D, n_small), lambda i, j: (layer, 0, 0), pipeline_mode=pl.Buffered(1)),
        ],
        out_specs=[
            pl.BlockSpec((tm, tn), lambda i, j: (i, j)),
            pl.BlockSpec((tm, n_small), lambda i, j: (i, 0)),
        ],
        out_shape=[
            jax.ShapeDtypeStruct((T, n_main), BF16),
            jax.ShapeDtypeStruct((T, n_small), F32),
        ],
        compiler_params=_cparams(("parallel", "arbitrary")),
        name="inproj",
    )(xg, ssq, w_main, w_small)


def _pool_kernel(cur_ref, prev_ref, w_ref, s_ref, o_ref, ext_ref, *, tm, blocks_per_seq):
    blk = pl.program_id(0) % blocks_per_seq
    cur = cur_ref[...].astype(F32)
    ext_ref[0:POOL_HALO, :] = jnp.where(blk == 0, 0.0, prev_ref[...].astype(F32))
    ext_ref[POOL_HALO:, :] = cur
    pos = blk * tm + lax.broadcasted_iota(jnp.int32, (tm, 1), 0)
    for gi, win in enumerate(POOL_WINDOWS):
        lo = gi * POOL_GROUP
        u = cur[:, lo:lo + POOL_GROUP]
        acc = u
        for s in range(1, win):
            acc = acc + ext_ref[POOL_HALO - s:POOL_HALO - s + tm, lo:lo + POOL_GROUP]
        count = jnp.minimum(pos + 1, win).astype(F32)
        p = (acc / count - u).astype(BF16)
        y = _dot(p, w_ref[gi]) * s_ref[:, lo:lo + POOL_GROUP]
        o_ref[:, lo:lo + POOL_GROUP] = y.astype(BF16)


def _pool(z, w_pool, s_pool, layer, seq, tm=512):
    T = z.shape[0]
    halo_per_tile = tm // POOL_HALO
    return pl.pallas_call(
        functools.partial(_pool_kernel, tm=tm, blocks_per_seq=seq // tm),
        grid=(T // tm,),
        in_specs=[
            pl.BlockSpec((tm, POOL_WIDTH), lambda i: (i, Z_POOL // POOL_WIDTH)),
            pl.BlockSpec((POOL_HALO, POOL_WIDTH),
                         lambda i: (jnp.maximum(i * halo_per_tile - 1, 0), Z_POOL // POOL_WIDTH)),
            pl.BlockSpec((None, len(POOL_WINDOWS), POOL_GROUP, POOL_GROUP), lambda i: (layer, 0, 0, 0)),
            pl.BlockSpec((None, 1, POOL_WIDTH), lambda i: (layer, 0, 0)),
        ],
        out_specs=pl.BlockSpec((tm, POOL_WIDTH), lambda i: (i, 0)),
        out_shape=jax.ShapeDtypeStruct((T, POOL_WIDTH), BF16),
        scratch_shapes=[pltpu.VMEM((tm + POOL_HALO, POOL_WIDTH), F32)],
        compiler_params=_cparams(("parallel",)),
        name="pool",
    )(z, z, w_pool, s_pool)


def _fcum_kernel(zs_ref, bf_ref, qx_ref, kx_ref, carry_ref, *, tc, blocks_per_seq):
    @pl.when(pl.program_id(0) % blocks_per_seq == 0)
    def _():
        carry_ref[...] = jnp.zeros_like(carry_ref)

    lf = _log_sigmoid(zs_ref[...] + bf_ref[...])
    c = _cumsum_rows(_tril_ones(tc), lf) + carry_ref[...]
    carry_ref[...] = c[tc - 1:tc, :]

    lane = lax.broadcasted_iota(jnp.int32, (tc, LANES), 1)
    ones_q = ((lane >= 3) & (lane < 6)).astype(F32)
    for h in range(FOX_HEADS):
        big = jnp.broadcast_to(c[:, ZS_F_LANE + h:ZS_F_LANE + h + 1], (tc, LANES)) * (FOX_DH ** 0.5)
        hi = big.astype(BF16).astype(F32)
        rest = big - hi
        mid = rest.astype(BF16).astype(F32)
        lo = rest - mid
        trip = jnp.where(lane % 3 == 0, hi, jnp.where(lane % 3 == 1, mid, lo))
        qx_ref[:, h * LANES:(h + 1) * LANES] = jnp.where(lane < 3, trip, ones_q).astype(BF16)
        kx_ref[:, h * LANES:(h + 1) * LANES] = jnp.where(
            lane < 3, 1.0, jnp.where(lane < 6, -trip, 0.0)).astype(BF16)


def _fcum(zs, bf_pad, layer, seq, tc=512):
    T = zs.shape[0]
    x_spec = pl.BlockSpec((tc, FOX_HEADS * LANES), lambda i: (i, 0))
    x_shape = jax.ShapeDtypeStruct((T, FOX_HEADS * LANES), BF16)
    return pl.pallas_call(
        functools.partial(_fcum_kernel, tc=tc, blocks_per_seq=seq // tc),
        grid=(T // tc,),
        in_specs=[
            pl.BlockSpec((tc, LANES), lambda i: (i, ZS_SMALL // LANES)),
            pl.BlockSpec((None, 1, LANES), lambda i: (layer, 0, 0)),
        ],
        out_specs=[x_spec, x_spec],
        out_shape=[x_shape, x_shape],
        scratch_shapes=[pltpu.VMEM((1, LANES), F32)],
        compiler_params=_cparams(("arbitrary",)),
        name="fcum",
    )(zs, bf_pad)


def _gla_kernel(q_ref, k_ref, v_ref, r_ref, a_ref, wa_ref, ba_ref, gg_ref, o_ref,
                st_ref, g_ref, *, tg, hp):
    @pl.when(pl.program_id(2) == 0)
    def _():
        st_ref[...] = jnp.zeros_like(st_ref)

    logits = _dot(a_ref[...].astype(BF16), wa_ref[...]) + ba_ref[...]
    g_ref[...] = _log_sigmoid(logits) * (1.0 / GLA_GATE_NORMALIZER)

    C, SB = GLA_CHUNK, GLA_SUB
    nsb = C // SB
    row = lax.broadcasted_iota(jnp.int32, (C, C), 0)
    col = lax.broadcasted_iota(jnp.int32, (C, C), 1)
    tril = _tril_ones(C)
    sdiff = row // SB - col // SB
    lane_sb = lax.broadcasted_iota(jnp.int32, (SB, C), 1)
    row_sb = lax.broadcasted_iota(jnp.int32, (SB, C), 0)
    gain = gg_ref[...]

    def rows4(parts):
        return jnp.concatenate([jnp.broadcast_to(p, (SB, GLA_DK)) for p in parts], axis=0)

    def one_head(hh, r0):
        kcols = slice(hh * GLA_DK, (hh + 1) * GLA_DK)
        vcols = slice(hh * GLA_DV, (hh + 1) * GLA_DV)
        q = q_ref[pl.ds(r0, C), kcols].astype(F32) * (GLA_DK ** -0.5)
        k = k_ref[pl.ds(r0, C), kcols].astype(F32)
        v = v_ref[pl.ds(r0, C), vcols]
        g = g_ref[pl.ds(r0, C), kcols]
        b = _cumsum_rows(tril, g)
        edge = [jnp.zeros((1, GLA_DK), F32)] + [b[SB * (i + 1) - 1:SB * (i + 1), :] for i in range(nsb)]
        zero = edge[0]
        rb = rows4(edge[0:nsb])
        rn = rows4(edge[1:nsb + 1])
        qt = q * jnp.exp(b - rb)
        kt = (k * jnp.exp(rn - b)).astype(BF16)
        a_off = jnp.where(sdiff == 1, _dot_nt(qt.astype(BF16), kt), 0.0)
        for s in range(2, nsb):
            d = rows4([zero] * s + [jnp.exp(edge[i] - edge[i - s + 1]) for i in range(s, nsb)])
            a_off = a_off + jnp.where(sdiff == s, _dot_nt((qt * d).astype(BF16), kt), 0.0)
        a_rows = []
        b2 = b * float(np.log2(np.e))
        for sb in range(nsb):
            sl = slice(SB * sb, SB * (sb + 1))
            q_s, k_s, b_s = q[sl], k[sl], b2[sl]
            acc = a_off[sl]
            for j in range(SB):
                e = jnp.exp2(b_s - b_s[j:j + 1])
                colv = jnp.sum(q_s * (k_s[j:j + 1] * e), axis=-1, keepdims=True)
                acc = jnp.where((lane_sb == SB * sb + j) & (row_sb >= j), colv, acc)
            a_rows.append(acc)
        a = jnp.concatenate(a_rows, axis=0).astype(BF16)

        st = st_ref[hh]
        o = _dot(a, v) + _dot_nt((q * jnp.exp(b)).astype(BF16), st.astype(BF16))
        b_last = edge[nsb]
        k_dec = (k * jnp.exp(b_last - b)).astype(BF16)
        st_ref[hh] = st * jnp.exp(b_last) + _dot_tn(v, k_dec)

        o = o * lax.rsqrt(jnp.mean(o * o, axis=-1, keepdims=True) + EPS) * gain
        r = r_ref[pl.ds(r0, C), vcols].astype(F32)
        o_ref[pl.ds(r0, C), vcols] = (o * (r * _sigmoid(r))).astype(BF16)

    def chunk(c, carry):
        r0 = pl.multiple_of(c * C, C)
        for hh in range(hp):
            one_head(hh, r0)
        return carry

    lax.fori_loop(0, tg // C, chunk, 0, unroll=2)


def _gla(z, zs, wa_t, b_alpha, g_gla, layer, batch, seq, tg=512, hp=4):
    T = z.shape[0]
    nb = seq // tg
    kw, vw = hp * GLA_DK, hp * GLA_DV
    qb, kb, vb, rb = Z_GQ // kw, Z_GK // kw, Z_GV // vw, Z_GR // vw
    return pl.pallas_call(
        functools.partial(_gla_kernel, tg=tg, hp=hp),
        grid=(batch, GLA_HEADS // hp, nb),
        in_specs=[
            pl.BlockSpec((tg, kw), lambda b, h, n: (b * nb + n, qb + h)),
            pl.BlockSpec((tg, kw), lambda b, h, n: (b * nb + n, kb + h)),
            pl.BlockSpec((tg, vw), lambda b, h, n: (b * nb + n, vb + h)),
            pl.BlockSpec((tg, vw), lambda b, h, n: (b * nb + n, rb + h)),
            pl.BlockSpec((tg, LANES), lambda b, h, n: (b * nb + n, ZS_SMALL // LANES)),
            pl.BlockSpec((None, LANES, kw), lambda b, h, n: (layer, 0, h)),
            pl.BlockSpec((None, 1, kw), lambda b, h, n: (layer, 0, h)),
            pl.BlockSpec((None, 1, GLA_DV), lambda b, h, n: (layer, 0, 0)),
        ],
        out_specs=pl.BlockSpec((tg, vw), lambda b, h, n: (b * nb + n, h)),
        out_shape=jax.ShapeDtypeStruct((T, GLA_V_W), BF16),
        scratch_shapes=[pltpu.VMEM((hp, GLA_DV, GLA_DK), F32), pltpu.VMEM((tg, kw), F32)],
        compiler_params=_cparams(("parallel", "parallel", "arbitrary")),
        name="gla",
    )(z, z, z, z, zs, wa_t, b_alpha, g_gla)


def _fox_kernel(qi_ref, ki_ref, q_ref, qx_ref, k_ref, kx_ref, v_ref, o_ref,
                m_ref, l_ref, acc_ref, qa_ref, s_ref, *, tq, tk, strip):
    t = pl.program_id(2)
    qi = qi_ref[t]
    ki = ki_ref[t]
    exp2_scale = (FOX_DH ** -0.5) * float(np.log2(np.e))

    @pl.when(ki == 0)
    def _():
        m_ref[...] = jnp.full_like(m_ref, -jnp.inf)
        l_ref[...] = jnp.zeros_like(l_ref)
        acc_ref[...] = jnp.zeros_like(acc_ref)
        qa_ref[:, 0:FOX_DH] = q_ref[...]
        qa_ref[:, FOX_DH:] = qx_ref[...]

    def step(diag):
        ka = jnp.concatenate([k_ref[...], kx_ref[...]], axis=1)
        v = v_ref[...]
        nstrip = tq // strip

        def width(c):
            return (c + 1) * strip if diag else tk

        def scores(c):
            s_ref[c % 2, :, :width(c)] = _dot_nt(qa_ref[c * strip:(c + 1) * strip, :], ka[:width(c)])

        scores(0)
        for c in range(nstrip):
            rows = slice(c * strip, (c + 1) * strip)
            ncol = width(c)
            if c + 1 < nstrip:
                scores(c + 1)
            s = s_ref[c % 2, :, :ncol]
            if diag:
                row = lax.broadcasted_iota(jnp.int32, (strip, ncol), 0) + c * strip
                col = lax.broadcasted_iota(jnp.int32, (strip, ncol), 1)
                s = jnp.where(col <= row, s, -jnp.inf)
            m_old = m_ref[rows, :]
            m_new = jnp.maximum(m_old, jnp.max(s, axis=-1, keepdims=True))
            p = jnp.exp2((s - m_new) * exp2_scale)
            alpha = jnp.exp2((m_old - m_new) * exp2_scale)
            l_ref[rows, :] = alpha * l_ref[rows, :] + jnp.sum(p, axis=-1, keepdims=True)
            acc_ref[rows, :] = alpha * acc_ref[rows, :] + _dot(p.astype(BF16), v[:ncol])
            m_ref[rows, :] = m_new

    @pl.when(ki < qi)
    def _():
        step(False)

    @pl.when(ki == qi)
    def _():
        step(True)
        o_ref[...] = (acc_ref[...] / l_ref[...]).astype(BF16)


def _fox(z, qx, kx, batch, seq, tq=2048, strip=512):
    T = z.shape[0]
    tk = tq
    nq = seq // tq
    pairs = [(qi, ki) for qi in range(nq) for ki in range(qi + 1)]
    qi_tab = jnp.asarray(np.array([p[0] for p in pairs], np.int32))
    ki_tab = jnp.asarray(np.array([p[1] for p in pairs], np.int32))
    qb, kb, vb = Z_FQ // FOX_DH, Z_FK // FOX_DH, Z_FV // FOX_DH
    grid_spec = pltpu.PrefetchScalarGridSpec(
        num_scalar_prefetch=2,
        grid=(batch, FOX_HEADS, len(pairs)),
        in_specs=[
            pl.BlockSpec((tq, FOX_DH), lambda b, h, t, qt, kt: (b * nq + qt[t], qb + h)),
            pl.BlockSpec((tq, LANES), lambda b, h, t, qt, kt: (b * nq + qt[t], h)),
            pl.BlockSpec((tk, FOX_DH), lambda b, h, t, qt, kt: (b * nq + kt[t], kb + h)),
            pl.BlockSpec((tk, LANES), lambda b, h, t, qt, kt: (b * nq + kt[t], h)),
            pl.BlockSpec((tk, FOX_DH), lambda b, h, t, qt, kt: (b * nq + kt[t], vb + h)),
        ],
        out_specs=pl.BlockSpec((tq, FOX_DH), lambda b, h, t, qt, kt: (b * nq + qt[t], h)),
        scratch_shapes=[
            pltpu.VMEM((tq, 1), F32), pltpu.VMEM((tq, 1), F32),
            pltpu.VMEM((tq, FOX_DH), F32), pltpu.VMEM((tq, FOX_DH + LANES), BF16),
            pltpu.VMEM((2, strip, tk), F32),
        ],
    )
    return pl.pallas_call(
        functools.partial(_fox_kernel, tq=tq, tk=tk, strip=strip),
        grid_spec=grid_spec,
        out_shape=jax.ShapeDtypeStruct((T, FOX_W), BF16),
        compiler_params=_cparams(("parallel", "parallel", "arbitrary")),
        name="fox",
    )(qi_tab, ki_tab, z, qx, z, kx, z)


def _merge_kernel(yp_ref, yg_ref, yf_ref, gl_ref, wp_ref, wg_ref, wf_ref,
                  u0_ref, u1_ref, u2_ref, b0_ref, b1_ref, b2_ref, o_ref):
    gl = gl_ref[...].astype(BF16)
    out = _sigmoid(_dot(gl, u0_ref[...]) + b0_ref[...]) * _dot(yp_ref[...], wp_ref[...])
    out = out + _sigmoid(_dot(gl, u1_ref[...]) + b1_ref[...]) * _dot(yg_ref[...], wg_ref[...])
    out = out + _sigmoid(_dot(gl, u2_ref[...]) + b2_ref[...]) * _dot(yf_ref[...], wf_ref[...])
    o_ref[...] = out.astype(BF16)


def _merge(yp, yg, yf, zs, wpp, wpg, wpf, wgu, b_gate, layer, tm=1024, tn=512):
    T = yp.shape[0]
    nj = D_MODEL // tn
    y_spec = pl.BlockSpec((tm, POOL_WIDTH), lambda i, j: (i, 0))
    w_spec = pl.BlockSpec((None, POOL_WIDTH, tn), lambda i, j: (layer, 0, j))

    def u_spec(br):
        return pl.BlockSpec((None, GATE_RANK, tn), lambda i, j: (layer, 0, br * nj + j))

    def b_spec(br):
        return pl.BlockSpec((None, 1, tn), lambda i, j: (layer, 0, br * nj + j))

    return pl.pallas_call(
        _merge_kernel,
        grid=(T // tm, nj),
        in_specs=[y_spec, y_spec, y_spec,
                  pl.BlockSpec((tm, GATE_RANK), lambda i, j: (i, ZS_GATE // GATE_RANK)),
                  w_spec, w_spec, w_spec,
                  u_spec(0), u_spec(1), u_spec(2), b_spec(0), b_spec(1), b_spec(2)],
        out_specs=pl.BlockSpec((tm, tn), lambda i, j: (i, j)),
        out_shape=jax.ShapeDtypeStruct((T, D_MODEL), BF16),
        compiler_params=_cparams(("parallel", "arbitrary")),
        name="merge",
    )(yp, yg, yf, zs, wpp, wpg, wpf, wgu, wgu, wgu, b_gate, b_gate, b_gate)


def _mm_res_kernel(a_ref, w_ref, res_ref, o_ref):
    o_ref[...] = res_ref[...] + _dot(a_ref[...], w_ref[...])


def _mm_res_norm_kernel(a_ref, w_ref, res_ref, g_ref, o_ref, xg_ref, ssq_ref):
    x = res_ref[...] + _dot(a_ref[...], w_ref[...])
    o_ref[...] = x
    xg_ref[...] = (x * g_ref[...]).astype(BF16)
    _accumulate_ssq(ssq_ref, jnp.sum(x * x, axis=-1, keepdims=True), pl.program_id(1))


def _mm_res(a, w, res, layer, tm, tn, name, gain=None, gain_layer=None):
    T = a.shape[0]
    K, N = w.shape[-2:]
    in_specs = [
        pl.BlockSpec((tm, K), lambda i, j: (i, 0)),
        pl.BlockSpec((None, K, tn), lambda i, j: (layer, 0, j)),
        pl.BlockSpec((tm, tn), lambda i, j: (i, j)),
    ]
    x_spec = pl.BlockSpec((tm, tn), lambda i, j: (i, j))
    x_shape = jax.ShapeDtypeStruct((T, N), F32)
    if gain is None:
        return pl.pallas_call(
            _mm_res_kernel, grid=(T // tm, N // tn), in_specs=in_specs, out_specs=x_spec,
            out_shape=x_shape, compiler_params=_cparams(("parallel", "arbitrary")), name=name,
        )(a, w, res)
    return pl.pallas_call(
        _mm_res_norm_kernel,
        grid=(T // tm, N // tn),
        in_specs=in_specs + [pl.BlockSpec((None, 1, tn), lambda i, j: (gain_layer, 0, j))],
        out_specs=[x_spec, x_spec, pl.BlockSpec((tm, LANES), lambda i, j: (i, 0))],
        out_shape=[x_shape, jax.ShapeDtypeStruct((T, N), BF16), jax.ShapeDtypeStruct((T, LANES), F32)],
        compiler_params=_cparams(("parallel", "arbitrary")),
        name=name,
    )(a, w, res, gain)


def _swiglu_tile(xg, rinv, wg, wu):
    gate = _dot(xg, wg) * rinv
    up = _dot(xg, wu) * rinv
    return (gate * _sigmoid(gate) * up).astype(BF16)


def _ffn_up_head_kernel(xg_ref, ssq_ref, wg_ref, wu_ref, o_ref, wgb_ref, wub_ref):
    wg = wg_ref[...].astype(BF16)
    wu = wu_ref[...].astype(BF16)
    wgb_ref[...] = wg
    wub_ref[...] = wu
    o_ref[...] = _swiglu_tile(xg_ref[...], _row_rinv(ssq_ref), wg, wu)


def _ffn_up_head(xg, ssq, wg32, wu32, layer, n_out, tm=1024, tn=256):
    T, D = xg.shape
    N = wg32.shape[-1]
    w32_spec = pl.BlockSpec((None, D, tn), lambda j: (layer, 0, j))
    wb_spec = pl.BlockSpec((D, tn), lambda j: (0, j))
    wb_shape = jax.ShapeDtypeStruct((D, N), BF16)
    return pl.pallas_call(
        _ffn_up_head_kernel,
        grid=(N // tn,),
        in_specs=[
            pl.BlockSpec((tm, D), lambda j: (0, 0)),
            pl.BlockSpec((tm, LANES), lambda j: (0, 0)),
            w32_spec, w32_spec,
        ],
        out_specs=[pl.BlockSpec((tm, tn), lambda j: (0, j)), wb_spec, wb_spec],
        out_shape=[jax.ShapeDtypeStruct((T, n_out), BF16), wb_shape, wb_shape],
        compiler_params=_cparams(("parallel",)),
        name="ffn_up_head",
    )(xg, ssq, wg32, wu32)


def _ffn_up_kernel(xg_ref, ssq_ref, wg_ref, wu_ref, hid_in_ref, o_ref, *, tail):
    del hid_in_ref
    rinv = _row_rinv(ssq_ref)
    last = pl.num_programs(1) - 1

    def tile(width):
        o_ref[:, :width] = _swiglu_tile(xg_ref[...], rinv, wg_ref[:, :width], wu_ref[:, :width])

    if tail == o_ref.shape[1]:
        tile(tail)
    else:
        pl.when(pl.program_id(1) != last)(lambda: tile(o_ref.shape[1]))
        pl.when(pl.program_id(1) == last)(lambda: tile(tail))


def _ffn_up(xg, ssq, wg32, wu32, layer, tm=1024, tn=512):
    T, D = xg.shape
    N = wg32.shape[-1]
    n_out = pl.cdiv(N, tn) * tn
    hid, wg, wu = _ffn_up_head(xg, ssq, wg32, wu32, layer, n_out, tm=tm)
    w_spec = pl.BlockSpec((D, tn), lambda i, j: (0, j))
    return pl.pallas_call(
        functools.partial(_ffn_up_kernel, tail=N - (pl.cdiv(N, tn) - 1) * tn),
        grid=(T // tm - 1, pl.cdiv(N, tn)),
        in_specs=[
            pl.BlockSpec((tm, D), lambda i, j: (i + 1, 0)),
            pl.BlockSpec((tm, LANES), lambda i, j: (i + 1, 0)),
            w_spec, w_spec,
            pl.BlockSpec(memory_space=pl.ANY),
        ],
        out_specs=pl.BlockSpec((tm, tn), lambda i, j: (i + 1, j)),
        out_shape=jax.ShapeDtypeStruct((T, n_out), BF16),
        input_output_aliases={4: 0},
        compiler_params=_cparams(("parallel", "arbitrary")),
        name="ffn_up",
    )(xg, ssq, wg, wu, hid)


def _norm_kernel(x_ref, g_ref, o_ref):
    o_ref[...] = _rms_rows(x_ref[...], g_ref[...])


def _final_norm(x, g, tm=512):
    T, D = x.shape
    return pl.pallas_call(
        _norm_kernel,
        grid=(T // tm,),
        in_specs=[pl.BlockSpec((tm, D), lambda i: (i, 0)), pl.BlockSpec((1, D), lambda i: (0, 0))],
        out_specs=pl.BlockSpec((tm, D), lambda i: (i, 0)),
        out_shape=jax.ShapeDtypeStruct((T, D), F32),
        compiler_params=_cparams(("parallel",)),
        name="final_norm",
    )(x, g)


def _split_offsets():
    sizes = (POOL_WIDTH, GLA_QK_W, GLA_QK_W, GLA_V_W, GLA_GATE_RANK, GLA_V_W,
             FOX_W, FOX_W, FOX_W, FOX_HEADS, GATE_RANK)
    offs = np.concatenate([[0], np.cumsum(sizes)])
    return [int(o) for o in offs]


def _prep_w_in(w_in):
    o = _split_offsets()
    a_lo, a_hi = o[4], o[5]
    f_lo, f_hi = o[9], o[10]
    gl_lo, gl_hi = o[10], o[11]
    wb = w_in.astype(BF16)
    main = jnp.concatenate([wb[..., :a_lo], wb[..., a_hi:f_lo]], axis=-1)
    pad = jnp.zeros(wb.shape[:-1] + (ZS_W - GATE_RANK - GLA_GATE_RANK - FOX_HEADS,), BF16)
    small = jnp.concatenate([wb[..., gl_lo:gl_hi], wb[..., a_lo:a_hi], wb[..., f_lo:f_hi], pad], axis=-1)
    return main, small


def kernel(x, g_mix, w_in, w_pool, s_pool, w_alpha, b_alpha, g_gla, b_f, w_gate_up, b_gate,
           w_proj_pool, w_proj_gla, w_proj_fox, w_o, g_ffn, w_ffn_gate, w_ffn_up, w_ffn_down, g_final):
    B, S, D = x.shape
    T = B * S
    L = g_mix.shape[0]
    xf = x.reshape(T, D)

    w_main, w_small = _prep_w_in(w_in)
    w_pool_b = w_pool.astype(BF16)
    wa = jnp.pad(w_alpha, ((0, 0), (ZS_A_LANE, LANES - ZS_A_LANE - GLA_GATE_RANK), (0, 0))).astype(BF16)
    bf_pad = jnp.pad(b_f, ((0, 0), (ZS_F_LANE, LANES - ZS_F_LANE - FOX_HEADS)))[:, None, :]
    wgu = w_gate_up.astype(BF16)
    wpp, wpg, wpf = (w.astype(BF16) for w in (w_proj_pool, w_proj_gla, w_proj_fox))
    wo = w_o.astype(BF16)
    wfd = w_ffn_down.astype(BF16)
    g_mix3, g_ffn3 = g_mix[:, None, :], g_ffn[:, None, :]
    s_pool3, b_alpha3, g_gla3, b_gate3 = (a[:, None, :] for a in (s_pool, b_alpha, g_gla, b_gate))

    xg, ssq = _norm_prep(xf, g_mix3, 0)
    for l in range(L):
        z, zs = _inproj(xg, ssq, w_main, w_small, l)
        y_pool = _pool(z, w_pool_b, s_pool3, l, S)
        qx, kx = _fcum(zs, bf_pad, l, S)
        y_gla = _gla(z, zs, wa, b_alpha3, g_gla3, l, B, S)
        y_fox = _fox(z, qx, kx, B, S)
        merged = _merge(y_pool, y_gla, y_fox, zs, wpp, wpg, wpf, wgu, b_gate3, l)
        xf, xg, ssq = _mm_res(merged, wo, xf, l, tm=1024, tn=512, name="out_proj", gain=g_ffn3, gain_layer=l)
        hid = _ffn_up(xg, ssq, w_ffn_gate, w_ffn_up, l)
        if l + 1 < L:
            xf, xg, ssq = _mm_res(hid, wfd, xf, l, tm=512, tn=512, name="ffn_down", gain=g_mix3, gain_layer=l + 1)
        else:
            xf = _mm_res(hid, wfd, xf, l, tm=512, tn=512, name="ffn_down")
    return _final_norm(xf, g_final[None, :]).reshape(B, S, D)
```

```python
import functools

import jax
import jax.numpy as jnp
import numpy as np
from jax import lax
from jax.experimental import pallas as pl
from jax.experimental.pallas import tpu as pltpu

F32 = jnp.float32
BF16 = jnp.bfloat16

D_MODEL = 4096
DEPTH = 4
EPS = 1e-6
POOL_WIDTH = 1024
POOL_WINDOWS = (2, 4, 8, 16)
POOL_GROUP = POOL_WIDTH // len(POOL_WINDOWS)
POOL_HALO = 16
GLA_HEADS = 4
GLA_DK = 128
GLA_DV = 256
GLA_QK_W = GLA_HEADS * GLA_DK
GLA_V_W = GLA_HEADS * GLA_DV
GLA_GATE_RANK = 16
GLA_GATE_NORMALIZER = 16.0
GLA_CHUNK = 64
GLA_SUB = 16
FOX_HEADS = 8
FOX_DH = 128
FOX_W = FOX_HEADS * FOX_DH
N_BRANCH = 3
GATE_RANK = 256
FFN_HIDDEN = ((8 * D_MODEL // 3 + 255) // 256) * 256

LANES = 128
Z_POOL, Z_GQ, Z_GK, Z_GV, Z_GR, Z_FQ, Z_FK, Z_FV = 0, 1024, 1536, 2048, 3072, 4096, 5120, 6144
Z_MAIN_W = 7168
ZS_GATE = 0
ZS_SMALL = GATE_RANK
ZS_A_LANE = 0
ZS_F_LANE = GLA_GATE_RANK
ZS_W = GATE_RANK + LANES

VMEM_LIMIT = 56 * 1024 * 1024
CAST_SLAB_ROWS = 64


def _cparams(sem):
    return pltpu.CompilerParams(dimension_semantics=sem, vmem_limit_bytes=VMEM_LIMIT)


def _dot(a, b):
    return jnp.dot(a, b, preferred_element_type=F32)


def _dot_nt(a, b):
    return lax.dot_general(a, b, (((1,), (1,)), ((), ())), preferred_element_type=F32)


def _dot_tn(a, b):
    return lax.dot_general(a, b, (((0,), (0,)), ((), ())), preferred_element_type=F32)


def _log_sigmoid(x):
    return jnp.minimum(x, 0.0) - jnp.log(1.0 + jnp.exp(-jnp.abs(x)))


def _sigmoid(x):
    return 1.0 / (1.0 + jnp.exp(-x))


def _tril_ones(n):
    row = lax.broadcasted_iota(jnp.int32, (n, n), 0)
    col = lax.broadcasted_iota(jnp.int32, (n, n), 1)
    return (row >= col).astype(BF16)


def _cumsum_rows(tril, x):
    hi = x.astype(BF16)
    rest = x - hi.astype(F32)
    mid = rest.astype(BF16)
    lo = (rest - mid.astype(F32)).astype(BF16)
    return _dot(tril, hi) + _dot(tril, mid) + _dot(tril, lo)


def _rms_rows(x, g):
    return x * lax.rsqrt(jnp.mean(x * x, axis=-1, keepdims=True) + EPS) * g


def _row_rinv(ssq_ref):
    return lax.rsqrt(ssq_ref[:, 0:1] * (1.0 / D_MODEL) + EPS)


def _accumulate_ssq(ssq_ref, row_sums, j):
    part = jnp.broadcast_to(row_sums, ssq_ref.shape)

    @pl.when(j == 0)
    def _():
        ssq_ref[...] = part

    @pl.when(j != 0)
    def _():
        ssq_ref[...] += part


def _norm_prep_kernel(x_ref, g_ref, xg_ref, ssq_ref):
    x = x_ref[...]
    xg_ref[...] = (x * g_ref[...]).astype(BF16)
    ssq_ref[...] = jnp.broadcast_to(jnp.sum(x * x, axis=-1, keepdims=True), ssq_ref.shape)


def _norm_prep(x, g, layer, tm=512):
    T, D = x.shape
    return pl.pallas_call(
        _norm_prep_kernel,
        grid=(T // tm,),
        in_specs=[pl.BlockSpec((tm, D), lambda i: (i, 0)),
                  pl.BlockSpec((None, 1, D), lambda i: (layer, 0, 0))],
        out_specs=[pl.BlockSpec((tm, D), lambda i: (i, 0)),
                   pl.BlockSpec((tm, LANES), lambda i: (i, 0))],
        out_shape=[jax.ShapeDtypeStruct((T, D), BF16), jax.ShapeDtypeStruct((T, LANES), F32)],
        compiler_params=_cparams(("parallel",)),
        name="norm_prep",
    )(x, g)


def _inproj_kernel(xg_ref, ssq_ref, w_ref, ws_ref, z_ref, zs_ref):
    rinv = _row_rinv(ssq_ref)

    @pl.when(pl.program_id(1) == 0)
    def _():
        zs_ref[...] = _dot(xg_ref[...], ws_ref[...]) * rinv

    z_ref[...] = (_dot(xg_ref[...], w_ref[...]) * rinv).astype(BF16)


def _inproj(xg, ssq, w_main, w_small, layer, tm=1024, tn=1024):
    T, D = xg.shape
    n_main = w_main.shape[-1]
    n_small = w_small.shape[-1]
    return pl.pallas_call(
        _inproj_kernel,
        grid=(T // tm, n_main // tn),
        in_specs=[
            pl.BlockSpec((tm, D), lambda i, j: (i, 0)),
            pl.BlockSpec((tm, LANES), lambda i, j: (i, 0)),
            pl.BlockSpec((None, D, tn), lambda i, j: (layer, 0, j)),
            pl.BlockSpec((None, D, n_small), lambda i, j: (layer, 0, 0), pipeline_mode=pl.Buffered(1)),
        ],
        out_specs=[
            pl.BlockSpec((tm, tn), lambda i, j: (i, j)),
            pl.BlockSpec((tm, n_small), lambda i, j: (i, 0)),
        ],
        out_shape=[
            jax.ShapeDtypeStruct((T, n_main), BF16),
            jax.ShapeDtypeStruct((T, n_small), F32),
        ],
        compiler_params=_cparams(("parallel", "arbitrary")),
        name="inproj",
    )(xg, ssq, w_main, w_small)


def _pool_kernel(cur_ref, prev_ref, w_ref, s_ref, o_ref, ext_ref, *, tm, blocks_per_seq):
    blk = pl.program_id(0) % blocks_per_seq
    cur = cur_ref[...].astype(F32)
    ext_ref[0:POOL_HALO, :] = jnp.where(blk == 0, 0.0, prev_ref[...].astype(F32))
    ext_ref[POOL_HALO:, :] = cur
    pos = blk * tm + lax.broadcasted_iota(jnp.int32, (tm, 1), 0)
    for gi, win in enumerate(POOL_WINDOWS):
        lo = gi * POOL_GROUP
        u = cur[:, lo:lo + POOL_GROUP]
        acc = u
        for s in range(1, win):
            acc = acc + ext_ref[POOL_HALO - s:POOL_HALO - s + tm, lo:lo + POOL_GROUP]
        count = jnp.minimum(pos + 1, win).astype(F32)
        p = (acc / count - u).astype(BF16)
        y = _dot(p, w_ref[gi]) * s_ref[:, lo:lo + POOL_GROUP]
        o_ref[:, lo:lo + POOL_GROUP] = y.astype(BF16)


def _pool(z, w_pool, s_pool, layer, seq, tm=512):
    T = z.shape[0]
    halo_per_tile = tm // POOL_HALO
    return pl.pallas_call(
        functools.partial(_pool_kernel, tm=tm, blocks_per_seq=seq // tm),
        grid=(T // tm,),
        in_specs=[
            pl.BlockSpec((tm, POOL_WIDTH), lambda i: (i, Z_POOL // POOL_WIDTH)),
            pl.BlockSpec((POOL_HALO, POOL_WIDTH),
                         lambda i: (jnp.maximum(i * halo_per_tile - 1, 0), Z_POOL // POOL_WIDTH)),
            pl.BlockSpec((None, len(POOL_WINDOWS), POOL_GROUP, POOL_GROUP), lambda i: (layer, 0, 0, 0)),
            pl.BlockSpec((None, 1, POOL_WIDTH), lambda i: (layer, 0, 0)),
        ],
        out_specs=pl.BlockSpec((tm, POOL_WIDTH), lambda i: (i, 0)),
        out_shape=jax.ShapeDtypeStruct((T, POOL_WIDTH), BF16),
        scratch_shapes=[pltpu.VMEM((tm + POOL_HALO, POOL_WIDTH), F32)],
        compiler_params=_cparams(("parallel",)),
        name="pool",
    )(z, z, w_pool, s_pool)


def _fcum_kernel(zs_ref, bf_ref, qx_ref, kx_ref, carry_ref, *, tc, blocks_per_seq):
    @pl.when(pl.program_id(0) % blocks_per_seq == 0)
    def _():
        carry_ref[...] = jnp.zeros_like(carry_ref)

    lf = _log_sigmoid(zs_ref[...] + bf_ref[...])
    c = _cumsum_rows(_tril_ones(tc), lf) + carry_ref[...]
    carry_ref[...] = c[tc - 1:tc, :]

    lane = lax.broadcasted_iota(jnp.int32, (tc, LANES), 1)
    ones_q = ((lane >= 3) & (lane < 6)).astype(F32)
    for h in range(FOX_HEADS):
        big = jnp.broadcast_to(c[:, ZS_F_LANE + h:ZS_F_LANE + h + 1], (tc, LANES)) * (FOX_DH ** 0.5)
        hi = big.astype(BF16).astype(F32)
        rest = big - hi
        mid = rest.astype(BF16).astype(F32)
        lo = rest - mid
        trip = jnp.where(lane % 3 == 0, hi, jnp.where(lane % 3 == 1, mid, lo))
        qx_ref[:, h * LANES:(h + 1) * LANES] = jnp.where(lane < 3, trip, ones_q).astype(BF16)
        kx_ref[:, h * LANES:(h + 1) * LANES] = jnp.where(
            lane < 3, 1.0, jnp.where(lane < 6, -trip, 0.0)).astype(BF16)


def _fcum(zs, bf_pad, layer, seq, tc=512):
    T = zs.shape[0]
    x_spec = pl.BlockSpec((tc, FOX_HEADS * LANES), lambda i: (i, 0))
    x_shape = jax.ShapeDtypeStruct((T, FOX_HEADS * LANES), BF16)
    return pl.pallas_call(
        functools.partial(_fcum_kernel, tc=tc, blocks_per_seq=seq // tc),
        grid=(T // tc,),
        in_specs=[
            pl.BlockSpec((tc, LANES), lambda i: (i, ZS_SMALL // LANES)),
            pl.BlockSpec((None, 1, LANES), lambda i: (layer, 0, 0)),
        ],
        out_specs=[x_spec, x_spec],
        out_shape=[x_shape, x_shape],
        scratch_shapes=[pltpu.VMEM((1, LANES), F32)],
        compiler_params=_cparams(("arbitrary",)),
        name="fcum",
    )(zs, bf_pad)


def _gla_kernel(q_ref, k_ref, v_ref, r_ref, a_ref, wa_ref, ba_ref, gg_ref, o_ref,
                st_ref, g_ref, *, tg, hp):
    @pl.when(pl.program_id(2) == 0)
    def _():
        st_ref[...] = jnp.zeros_like(st_ref)

    logits = _dot(a_ref[...].astype(BF16), wa_ref[...]) + ba_ref[...]
    g_ref[...] = _log_sigmoid(logits) * (1.0 / GLA_GATE_NORMALIZER)

    C, SB = GLA_CHUNK, GLA_SUB
    nsb = C // SB
    row = lax.broadcasted_iota(jnp.int32, (C, C), 0)
    col = lax.broadcasted_iota(jnp.int32, (C, C), 1)
    tril = _tril_ones(C)
    sdiff = row // SB - col // SB
    lane_sb = lax.broadcasted_iota(jnp.int32, (SB, C), 1)
    row_sb = lax.broadcasted_iota(jnp.int32, (SB, C), 0)
    gain = gg_ref[...]

    def rows4(parts):
        return jnp.concatenate([jnp.broadcast_to(p, (SB, GLA_DK)) for p in parts], axis=0)

    def one_head(hh, r0):
        kcols = slice(hh * GLA_DK, (hh + 1) * GLA_DK)
        vcols = slice(hh * GLA_DV, (hh + 1) * GLA_DV)
        q = q_ref[pl.ds(r0, C), kcols].astype(F32) * (GLA_DK ** -0.5)
        k = k_ref[pl.ds(r0, C), kcols].astype(F32)
        v = v_ref[pl.ds(r0, C), vcols]
        g = g_ref[pl.ds(r0, C), kcols]
        b = _cumsum_rows(tril, g)
        edge = [jnp.zeros((1, GLA_DK), F32)] + [b[SB * (i + 1) - 1:SB * (i + 1), :] for i in range(nsb)]
        zero = edge[0]
        rb = rows4(edge[0:nsb])
        rn = rows4(edge[1:nsb + 1])
        qt = q * jnp.exp(b - rb)
        kt = (k * jnp.exp(rn - b)).astype(BF16)
        a_off = jnp.where(sdiff == 1, _dot_nt(qt.astype(BF16), kt), 0.0)
        for s in range(2, nsb):
            d = rows4([zero] * s + [jnp.exp(edge[i] - edge[i - s + 1]) for i in range(s, nsb)])
            a_off = a_off + jnp.where(sdiff == s, _dot_nt((qt * d).astype(BF16), kt), 0.0)
        a_rows = []
        b2 = b * float(np.log2(np.e))
        for sb in range(nsb):
            sl = slice(SB * sb, SB * (sb + 1))
            q_s, k_s, b_s = q[sl], k[sl], b2[sl]
            acc = a_off[sl]
            for j in range(SB):
                e = jnp.exp2(b_s - b_s[j:j + 1])
                colv = jnp.sum(q_s * (k_s[j:j + 1] * e), axis=-1, keepdims=True)
                acc = jnp.where((lane_sb == SB * sb + j) & (row_sb >= j), colv, acc)
            a_rows.append(acc)
        a = jnp.concatenate(a_rows, axis=0).astype(BF16)

        st = st_ref[hh]
        o = _dot(a, v) + _dot_nt((q * jnp.exp(b)).astype(BF16), st.astype(BF16))
        b_last = edge[nsb]
        k_dec = (k * jnp.exp(b_last - b)).astype(BF16)
        st_ref[hh] = st * jnp.exp(b_last) + _dot_tn(v, k_dec)

        o = o * lax.rsqrt(jnp.mean(o * o, axis=-1, keepdims=True) + EPS) * gain
        r = r_ref[pl.ds(r0, C), vcols].astype(F32)
        o_ref[pl.ds(r0, C), vcols] = (o * (r * _sigmoid(r))).astype(BF16)

    def chunk(c, carry):
        r0 = pl.multiple_of(c * C, C)
        for hh in range(hp):
            one_head(hh, r0)
        return carry

    lax.fori_loop(0, tg // C, chunk, 0, unroll=2)


def _gla(z, zs, wa_t, b_alpha, g_gla, layer, batch, seq, tg=512, hp=4):
    T = z.shape[0]
    nb = seq // tg
    kw, vw = hp * GLA_DK, hp * GLA_DV
    qb, kb, vb, rb = Z_GQ // kw, Z_GK // kw, Z_GV // vw, Z_GR // vw
    return pl.pallas_call(
        functools.partial(_gla_kernel, tg=tg, hp=hp),
        grid=(batch, GLA_HEADS // hp, nb),
        in_specs=[
            pl.BlockSpec((tg, kw), lambda b, h, n: (b * nb + n, qb + h)),
            pl.BlockSpec((tg, kw), lambda b, h, n: (b * nb + n, kb + h)),
            pl.BlockSpec((tg, vw), lambda b, h, n: (b * nb + n, vb + h)),
            pl.BlockSpec((tg, vw), lambda b, h, n: (b * nb + n, rb + h)),
            pl.BlockSpec((tg, LANES), lambda b, h, n: (b * nb + n, ZS_SMALL // LANES)),
            pl.BlockSpec((None, LANES, kw), lambda b, h, n: (layer, 0, h)),
            pl.BlockSpec((None, 1, kw), lambda b, h, n: (layer, 0, h)),
            pl.BlockSpec((None, 1, GLA_DV), lambda b, h, n: (layer, 0, 0)),
        ],
        out_specs=pl.BlockSpec((tg, vw), lambda b, h, n: (b * nb + n, h)),
        out_shape=jax.ShapeDtypeStruct((T, GLA_V_W), BF16),
        scratch_shapes=[pltpu.VMEM((hp, GLA_DV, GLA_DK), F32), pltpu.VMEM((tg, kw), F32)],
        compiler_params=_cparams(("parallel", "parallel", "arbitrary")),
        name="gla",
    )(z, z, z, z, zs, wa_t, b_alpha, g_gla)


def _fox_kernel(qi_ref, ki_ref, q_ref, qx_ref, k_ref, kx_ref, v_ref, o_ref,
                m_ref, l_ref, acc_ref, qa_ref, s_ref, *, tq, tk, strip):
    t = pl.program_id(2)
    qi = qi_ref[t]
    ki = ki_ref[t]
    exp2_scale = (FOX_DH ** -0.5) * float(np.log2(np.e))

    @pl.when(ki == 0)
    def _():
        m_ref[...] = jnp.full_like(m_ref, -jnp.inf)
        l_ref[...] = jnp.zeros_like(l_ref)
        acc_ref[...] = jnp.zeros_like(acc_ref)
        qa_ref[:, 0:FOX_DH] = q_ref[...]
        qa_ref[:, FOX_DH:] = qx_ref[...]

    def step(diag):
        ka = jnp.concatenate([k_ref[...], kx_ref[...]], axis=1)
        v = v_ref[...]
        nstrip = tq // strip

        def width(c):
            return (c + 1) * strip if diag else tk

        def scores(c):
            s_ref[c % 2, :, :width(c)] = _dot_nt(qa_ref[c * strip:(c + 1) * strip, :], ka[:width(c)])

        scores(0)
        for c in range(nstrip):
            rows = slice(c * strip, (c + 1) * strip)
            ncol = width(c)
            if c + 1 < nstrip:
                scores(c + 1)
            s = s_ref[c % 2, :, :ncol]
            if diag:
                row = lax.broadcasted_iota(jnp.int32, (strip, ncol), 0) + c * strip
                col = lax.broadcasted_iota(jnp.int32, (strip, ncol), 1)
                s = jnp.where(col <= row, s, -jnp.inf)
            m_old = m_ref[rows, :]
            m_new = jnp.maximum(m_old, jnp.max(s, axis=-1, keepdims=True))
            p = jnp.exp2((s - m_new) * exp2_scale)
            alpha = jnp.exp2((m_old - m_new) * exp2_scale)
            l_ref[rows, :] = alpha * l_ref[rows, :] + jnp.sum(p, axis=-1, keepdims=True)
            acc_ref[rows, :] = alpha * acc_ref[rows, :] + _dot(p.astype(BF16), v[:ncol])
            m_ref[rows, :] = m_new

    @pl.when(ki < qi)
    def _():
        step(False)

    @pl.when(ki == qi)
    def _():
        step(True)
        o_ref[...] = (acc_ref[...] / l_ref[...]).astype(BF16)


def _fox(z, qx, kx, batch, seq, tq=2048, strip=512):
    T = z.shape[0]
    tk = tq
    nq = seq // tq
    pairs = [(qi, ki) for qi in range(nq) for ki in range(qi + 1)]
    qi_tab = jnp.asarray(np.array([p[0] for p in pairs], np.int32))
    ki_tab = jnp.asarray(np.array([p[1] for p in pairs], np.int32))
    qb, kb, vb = Z_FQ // FOX_DH, Z_FK // FOX_DH, Z_FV // FOX_DH
    grid_spec = pltpu.PrefetchScalarGridSpec(
        num_scalar_prefetch=2,
        grid=(batch, FOX_HEADS, len(pairs)),
        in_specs=[
            pl.BlockSpec((tq, FOX_DH), lambda b, h, t, qt, kt: (b * nq + qt[t], qb + h)),
            pl.BlockSpec((tq, LANES), lambda b, h, t, qt, kt: (b * nq + qt[t], h)),
            pl.BlockSpec((tk, FOX_DH), lambda b, h, t, qt, kt: (b * nq + kt[t], kb + h)),
            pl.BlockSpec((tk, LANES), lambda b, h, t, qt, kt: (b * nq + kt[t], h)),
            pl.BlockSpec((tk, FOX_DH), lambda b, h, t, qt, kt: (b * nq + kt[t], vb + h)),
        ],
        out_specs=pl.BlockSpec((tq, FOX_DH), lambda b, h, t, qt, kt: (b * nq + qt[t], h)),
        scratch_shapes=[
            pltpu.VMEM((tq, 1), F32), pltpu.VMEM((tq, 1), F32),
            pltpu.VMEM((tq, FOX_DH), F32), pltpu.VMEM((tq, FOX_DH + LANES), BF16),
            pltpu.VMEM((2, strip, tk), F32),
        ],
    )
    return pl.pallas_call(
        functools.partial(_fox_kernel, tq=tq, tk=tk, strip=strip),
        grid_spec=grid_spec,
        out_shape=jax.ShapeDtypeStruct((T, FOX_W), BF16),
        compiler_params=_cparams(("parallel", "parallel", "arbitrary")),
        name="fox",
    )(qi_tab, ki_tab, z, qx, z, kx, z)


def _merge_kernel(yp_ref, yg_ref, yf_ref, gl_ref, wp_ref, wg_ref, wf_ref,
                  u0_ref, u1_ref, u2_ref, b0_ref, b1_ref, b2_ref, o_ref):
    gl = gl_ref[...].astype(BF16)
    out = _sigmoid(_dot(gl, u0_ref[...]) + b0_ref[...]) * _dot(yp_ref[...], wp_ref[...])
    out = out + _sigmoid(_dot(gl, u1_ref[...]) + b1_ref[...]) * _dot(yg_ref[...], wg_ref[...])
    out = out + _sigmoid(_dot(gl, u2_ref[...]) + b2_ref[...]) * _dot(yf_ref[...], wf_ref[...])
    o_ref[...] = out.astype(BF16)


def _merge(yp, yg, yf, zs, wpp, wpg, wpf, wgu, b_gate, layer, tm=1024, tn=1024):
    T = yp.shape[0]
    nj = D_MODEL // tn
    y_spec = pl.BlockSpec((tm, POOL_WIDTH), lambda i, j: (i, 0))
    w_spec = pl.BlockSpec((None, POOL_WIDTH, tn), lambda i, j: (layer, 0, j))

    def u_spec(br):
        return pl.BlockSpec((None, GATE_RANK, tn), lambda i, j: (layer, 0, br * nj + j))

    def b_spec(br):
        return pl.BlockSpec((None, 1, tn), lambda i, j: (layer, 0, br * nj + j))

    return pl.pallas_call(
        _merge_kernel,
        grid=(T // tm, nj),
        in_specs=[y_spec, y_spec, y_spec,
                  pl.BlockSpec((tm, GATE_RANK), lambda i, j: (i, ZS_GATE // GATE_RANK)),
                  w_spec, w_spec, w_spec,
                  u_spec(0), u_spec(1), u_spec(2), b_spec(0), b_spec(1), b_spec(2)],
        out_specs=pl.BlockSpec((tm, tn), lambda i, j: (i, j)),
        out_shape=jax.ShapeDtypeStruct((T, D_MODEL), BF16),
        compiler_params=_cparams(("parallel", "arbitrary")),
        name="merge",
    )(yp, yg, yf, zs, wpp, wpg, wpf, wgu, wgu, wgu, b_gate, b_gate, b_gate)


def _mm_res_kernel(a_ref, w_ref, res_ref, o_ref):
    o_ref[...] = res_ref[...] + _dot(a_ref[...], w_ref[...])


def _mm_res_norm_kernel(a_ref, w_ref, res_ref, g_ref, o_ref, xg_ref, ssq_ref):
    x = res_ref[...] + _dot(a_ref[...], w_ref[...])
    o_ref[...] = x
    xg_ref[...] = (x * g_ref[...]).astype(BF16)
    _accumulate_ssq(ssq_ref, jnp.sum(x * x, axis=-1, keepdims=True), pl.program_id(1))


def _mm_res(a, w, res, layer, tm, tn, name, gain=None, gain_layer=None):
    T = a.shape[0]
    K, N = w.shape[-2:]
    if w.ndim == 3:
        w_spec = pl.BlockSpec((None, K, tn), lambda i, j: (layer, 0, j))
    else:
        w_spec = pl.BlockSpec((K, tn), lambda i, j: (0, j))
    in_specs = [
        pl.BlockSpec((tm, K), lambda i, j: (i, 0)),
        w_spec,
        pl.BlockSpec((tm, tn), lambda i, j: (i, j)),
    ]
    x_spec = pl.BlockSpec((tm, tn), lambda i, j: (i, j))
    x_shape = jax.ShapeDtypeStruct((T, N), F32)
    if gain is None:
        return pl.pallas_call(
            _mm_res_kernel, grid=(T // tm, N // tn), in_specs=in_specs, out_specs=x_spec,
            out_shape=x_shape, compiler_params=_cparams(("parallel", "arbitrary")), name=name,
        )(a, w, res)
    return pl.pallas_call(
        _mm_res_norm_kernel,
        grid=(T // tm, N // tn),
        in_specs=in_specs + [pl.BlockSpec((None, 1, tn), lambda i, j: (gain_layer, 0, j))],
        out_specs=[x_spec, x_spec, pl.BlockSpec((tm, LANES), lambda i, j: (i, 0))],
        out_shape=[x_shape, jax.ShapeDtypeStruct((T, N), BF16), jax.ShapeDtypeStruct((T, LANES), F32)],
        compiler_params=_cparams(("parallel", "arbitrary")),
        name=name,
    )(a, w, res, gain)


def _swiglu_tile(xg, rinv, wg, wu):
    gate = _dot(xg, wg) * rinv
    up = _dot(xg, wu) * rinv
    return (gate * _sigmoid(gate) * up).astype(BF16)


def _ffn_up_head_kernel(xg_ref, ssq_ref, wg_ref, wu_ref, o_ref, wgb_ref, wub_ref):
    wg = wg_ref[...].astype(BF16)
    wu = wu_ref[...].astype(BF16)
    wgb_ref[...] = wg
    wub_ref[...] = wu
    o_ref[...] = _swiglu_tile(xg_ref[...], _row_rinv(ssq_ref), wg, wu)


def _ffn_up_head(xg, ssq, wg32, wu32, layer, n_out, tm=1024, tn=256):
    T, D = xg.shape
    N = wg32.shape[-1]
    w32_spec = pl.BlockSpec((None, D, tn), lambda j: (layer, 0, j))
    wb_spec = pl.BlockSpec((D, tn), lambda j: (0, j))
    wb_shape = jax.ShapeDtypeStruct((D, N), BF16)
    return pl.pallas_call(
        _ffn_up_head_kernel,
        grid=(N // tn,),
        in_specs=[
            pl.BlockSpec((tm, D), lambda j: (0, 0)),
            pl.BlockSpec((tm, LANES), lambda j: (0, 0)),
            w32_spec, w32_spec,
        ],
        out_specs=[pl.BlockSpec((tm, tn), lambda j: (0, j)), wb_spec, wb_spec],
        out_shape=[jax.ShapeDtypeStruct((T, n_out), BF16), wb_shape, wb_shape],
        compiler_params=_cparams(("parallel",)),
        name="ffn_up_head",
    )(xg, ssq, wg32, wu32)


def _ffn_up_kernel(xg_ref, ssq_ref, wg_ref, wu_ref, wd_ref, hid_in_ref, o_ref, wdb_ref, *, tail):
    del hid_in_ref
    wdb_ref[...] = wd_ref[...].astype(BF16)
    rinv = _row_rinv(ssq_ref)
    last = pl.num_programs(1) - 1

    def tile(width):
        o_ref[:, :width] = _swiglu_tile(xg_ref[...], rinv, wg_ref[:, :width], wu_ref[:, :width])

    if tail == o_ref.shape[1]:
        tile(tail)
    else:
        pl.when(pl.program_id(1) != last)(lambda: tile(o_ref.shape[1]))
        pl.when(pl.program_id(1) == last)(lambda: tile(tail))


def _ffn_up(xg, ssq, wg32, wu32, wd32, layer, tm=1024, tn=512):
    T, D = xg.shape
    N = wg32.shape[-1]
    nj = pl.cdiv(N, tn)
    units, rem = divmod(wd32.shape[-2], CAST_SLAB_ROWS)
    assert rem == 0
    nslab = max(d for d in range(1, units + 1) if units % d == 0 and d <= (T // tm - 1) * nj)
    slab_rows = wd32.shape[-2] // nslab

    def slab(i, j):
        return jnp.minimum(i * nj + j, nslab - 1)

    n_out = pl.cdiv(N, tn) * tn
    hid, wg, wu = _ffn_up_head(xg, ssq, wg32, wu32, layer, n_out, tm=tm)
    w_spec = pl.BlockSpec((D, tn), lambda i, j: (0, j))
    return pl.pallas_call(
        functools.partial(_ffn_up_kernel, tail=N - (pl.cdiv(N, tn) - 1) * tn),
        grid=(T // tm - 1, pl.cdiv(N, tn)),
        in_specs=[
            pl.BlockSpec((tm, D), lambda i, j: (i + 1, 0)),
            pl.BlockSpec((tm, LANES), lambda i, j: (i + 1, 0)),
            w_spec, w_spec,
            pl.BlockSpec((None, slab_rows, D), lambda i, j: (layer, slab(i, j), 0)),
            pl.BlockSpec(memory_space=pl.ANY),
        ],
        out_specs=[pl.BlockSpec((tm, tn), lambda i, j: (i + 1, j)),
                   pl.BlockSpec((slab_rows, D), lambda i, j: (slab(i, j), 0))],
        out_shape=[jax.ShapeDtypeStruct((T, n_out), BF16),
                   jax.ShapeDtypeStruct(wd32.shape[-2:], BF16)],
        input_output_aliases={5: 0},
        compiler_params=_cparams(("arbitrary", "arbitrary")),
        name="ffn_up",
    )(xg, ssq, wg, wu, wd32, hid)


def _norm_kernel(x_ref, g_ref, o_ref):
    o_ref[...] = _rms_rows(x_ref[...], g_ref[...])


def _final_norm(x, g, tm=512):
    T, D = x.shape
    return pl.pallas_call(
        _norm_kernel,
        grid=(T // tm,),
        in_specs=[pl.BlockSpec((tm, D), lambda i: (i, 0)), pl.BlockSpec((1, D), lambda i: (0, 0))],
        out_specs=pl.BlockSpec((tm, D), lambda i: (i, 0)),
        out_shape=jax.ShapeDtypeStruct((T, D), F32),
        compiler_params=_cparams(("parallel",)),
        name="final_norm",
    )(x, g)


def _split_offsets():
    sizes = (POOL_WIDTH, GLA_QK_W, GLA_QK_W, GLA_V_W, GLA_GATE_RANK, GLA_V_W,
             FOX_W, FOX_W, FOX_W, FOX_HEADS, GATE_RANK)
    offs = np.concatenate([[0], np.cumsum(sizes)])
    return [int(o) for o in offs]


def _prep_w_in(w_in):
    o = _split_offsets()
    a_lo, a_hi = o[4], o[5]
    f_lo, f_hi = o[9], o[10]
    gl_lo, gl_hi = o[10], o[11]
    wb = lax.optimization_barrier(w_in.astype(BF16))
    main = jnp.concatenate([wb[..., :a_lo], wb[..., a_hi:f_lo]], axis=-1)
    pad = jnp.zeros(wb.shape[:-1] + (ZS_W - GATE_RANK - GLA_GATE_RANK - FOX_HEADS,), BF16)
    small = jnp.concatenate([wb[..., gl_lo:gl_hi], wb[..., a_lo:a_hi], wb[..., f_lo:f_hi], pad], axis=-1)
    return main, small


def kernel(x, g_mix, w_in, w_pool, s_pool, w_alpha, b_alpha, g_gla, b_f, w_gate_up, b_gate,
           w_proj_pool, w_proj_gla, w_proj_fox, w_o, g_ffn, w_ffn_gate, w_ffn_up, w_ffn_down, g_final):
    B, S, D = x.shape
    T = B * S
    L = g_mix.shape[0]
    xf = x.reshape(T, D)

    w_main, w_small = _prep_w_in(w_in)
    w_pool_b = w_pool.astype(BF16)
    wa = jnp.pad(w_alpha, ((0, 0), (ZS_A_LANE, LANES - ZS_A_LANE - GLA_GATE_RANK), (0, 0))).astype(BF16)
    bf_pad = jnp.pad(b_f, ((0, 0), (ZS_F_LANE, LANES - ZS_F_LANE - FOX_HEADS)))[:, None, :]
    wgu = w_gate_up.astype(BF16)
    wpp, wpg, wpf = (w.astype(BF16) for w in (w_proj_pool, w_proj_gla, w_proj_fox))
    wo = w_o.astype(BF16)
    g_mix3, g_ffn3 = g_mix[:, None, :], g_ffn[:, None, :]
    s_pool3, b_alpha3, g_gla3, b_gate3 = (a[:, None, :] for a in (s_pool, b_alpha, g_gla, b_gate))

    xg, ssq = _norm_prep(xf, g_mix3, 0)
    for l in range(L):
        z, zs = _inproj(xg, ssq, w_main, w_small, l)
        y_pool = _pool(z, w_pool_b, s_pool3, l, S)
        qx, kx = _fcum(zs, bf_pad, l, S)
        y_gla = _gla(z, zs, wa, b_alpha3, g_gla3, l, B, S)
        y_fox = _fox(z, qx, kx, B, S)
        merged = _merge(y_pool, y_gla, y_fox, zs, wpp, wpg, wpf, wgu, b_gate3, l)
        xf, xg, ssq = _mm_res(merged, wo, xf, l, tm=1024, tn=512, name="out_proj", gain=g_ffn3, gain_layer=l)
        hid, wfd = _ffn_up(xg, ssq, w_ffn_gate, w_ffn_up, w_ffn_down, l)
        if l + 1 < L:
            xf, xg, ssq = _mm_res(hid, wfd, xf, l, tm=512, tn=512, name="ffn_down", gain=g_mix3, gain_layer=l + 1)
        else:
            xf = _mm_res(hid, wfd, xf, l, tm=512, tn=512, name="ffn_down")
    return _final_norm(xf, g_final[None, :]).reshape(B, S, D)
```

```python
import functools

import jax
import jax.numpy as jnp
import numpy as np
from jax import lax
from jax.experimental import pallas as pl
from jax.experimental.pallas import tpu as pltpu

F32 = jnp.float32
BF16 = jnp.bfloat16

D_MODEL = 4096
DEPTH = 4
EPS = 1e-6
POOL_WIDTH = 1024
POOL_WINDOWS = (2, 4, 8, 16)
POOL_GROUP = POOL_WIDTH // len(POOL_WINDOWS)
POOL_HALO = 32
GLA_HEADS = 4
GLA_DK = 128
GLA_DV = 256
GLA_QK_W = GLA_HEADS * GLA_DK
GLA_V_W = GLA_HEADS * GLA_DV
GLA_GATE_RANK = 16
GLA_GATE_NORMALIZER = 16.0
GLA_CHUNK = 64
GLA_SUB = 16
FOX_HEADS = 8
FOX_DH = 128
FOX_W = FOX_HEADS * FOX_DH
N_BRANCH = 3
GATE_RANK = 256
FFN_HIDDEN = ((8 * D_MODEL // 3 + 255) // 256) * 256

LANES = 128
Z_POOL, Z_GQ, Z_GK, Z_GV, Z_GR, Z_FQ, Z_FK, Z_FV = 0, 1024, 1536, 2048, 3072, 4096, 5120, 6144
Z_MAIN_W = 7168
ZS_GATE = 0
ZS_SMALL = GATE_RANK
ZS_A_LANE = 0
ZS_F_LANE = GLA_GATE_RANK
ZS_W = GATE_RANK + LANES

VMEM_LIMIT = 56 * 1024 * 1024
CAST_SLAB_ROWS = 64


def _cparams(sem):
    return pltpu.CompilerParams(dimension_semantics=sem, vmem_limit_bytes=VMEM_LIMIT)


def _dot(a, b):
    return jnp.dot(a, b, preferred_element_type=F32)


def _dot_nt(a, b):
    return lax.dot_general(a, b, (((1,), (1,)), ((), ())), preferred_element_type=F32)


def _dot_tn(a, b):
    return lax.dot_general(a, b, (((0,), (0,)), ((), ())), preferred_element_type=F32)


def _log_sigmoid(x):
    return jnp.minimum(x, 0.0) - jnp.log(1.0 + jnp.exp(-jnp.abs(x)))


def _sigmoid(x):
    return 1.0 / (1.0 + jnp.exp(-x))


def _tril_ones(n):
    row = lax.broadcasted_iota(jnp.int32, (n, n), 0)
    col = lax.broadcasted_iota(jnp.int32, (n, n), 1)
    return (row >= col).astype(BF16)


def _cumsum_rows(tril, x):
    hi = x.astype(BF16)
    rest = x - hi.astype(F32)
    mid = rest.astype(BF16)
    lo = (rest - mid.astype(F32)).astype(BF16)
    return _dot(tril, hi) + _dot(tril, mid) + _dot(tril, lo)


def _rms_rows(x, g):
    return x * lax.rsqrt(jnp.mean(x * x, axis=-1, keepdims=True) + EPS) * g


def _cast_rider(w32, layer, steps, flat_step):
    R, C = w32.shape[-2:]
    units, rem = divmod(R, CAST_SLAB_ROWS)
    assert rem == 0
    nslab = max(d for d in range(1, units + 1) if units % d == 0 and d <= steps)
    rows = R // nslab

    def slab(*g):
        return jnp.minimum(flat_step(*g), nslab - 1)

    return (pl.BlockSpec((None, rows, C), lambda *g: (layer, slab(*g), 0)),
            pl.BlockSpec((rows, C), lambda *g: (slab(*g), 0)),
            jax.ShapeDtypeStruct((R, C), BF16), nslab)


def _ride_casts(srcs, dsts, nslabs, step):
    for src, dst, n in zip(srcs, dsts, nslabs):
        @pl.when(step < n)
        def _():
            dst[...] = src[...].astype(BF16)


def _row_rinv(ssq_ref):
    return lax.rsqrt(ssq_ref[:, 0:1] * (1.0 / D_MODEL) + EPS)


def _accumulate_ssq(ssq_ref, row_sums, j):
    part = jnp.broadcast_to(row_sums, ssq_ref.shape)

    @pl.when(j == 0)
    def _():
        ssq_ref[...] = part

    @pl.when(j != 0)
    def _():
        ssq_ref[...] += part


def _norm_prep_kernel(x_ref, g_ref, xg_ref, ssq_ref):
    x = x_ref[...]
    xg_ref[...] = (x * g_ref[...]).astype(BF16)
    ssq_ref[...] = jnp.broadcast_to(jnp.sum(x * x, axis=-1, keepdims=True), ssq_ref.shape)


def _norm_prep(x, g, layer, tm=512):
    T, D = x.shape
    return pl.pallas_call(
        _norm_prep_kernel,
        grid=(T // tm,),
        in_specs=[pl.BlockSpec((tm, D), lambda i: (i, 0)),
                  pl.BlockSpec((None, 1, D), lambda i: (layer, 0, 0))],
        out_specs=[pl.BlockSpec((tm, D), lambda i: (i, 0)),
                   pl.BlockSpec((tm, LANES), lambda i: (i, 0))],
        out_shape=[jax.ShapeDtypeStruct((T, D), BF16), jax.ShapeDtypeStruct((T, LANES), F32)],
        compiler_params=_cparams(("parallel",)),
        name="norm_prep",
    )(x, g)


def _inproj_kernel(xg_ref, ssq_ref, w_ref, ws_ref, wo_ref, z_ref, zs_ref, wob_ref, *, wo_slabs):
    _ride_casts([wo_ref], [wob_ref], [wo_slabs], pl.program_id(0) * pl.num_programs(1) + pl.program_id(1))
    rinv = _row_rinv(ssq_ref)

    @pl.when(pl.program_id(1) == 0)
    def _():
        zs_ref[...] = _dot(xg_ref[...], ws_ref[...]) * rinv

    z_ref[...] = (_dot(xg_ref[...], w_ref[...]) * rinv).astype(BF16)


def _inproj(xg, ssq, w_main, w_small, wo32, layer, tm=1024, tn=1024):
    T, D = xg.shape
    n_main = w_main.shape[-1]
    n_small = w_small.shape[-1]
    nj = n_main // tn
    wo_in, wo_out, wo_shape, wo_slabs = _cast_rider(wo32, layer, (T // tm) * nj, lambda i, j: i * nj + j)
    return pl.pallas_call(
        functools.partial(_inproj_kernel, wo_slabs=wo_slabs),
        grid=(T // tm, n_main // tn),
        in_specs=[
            pl.BlockSpec((tm, D), lambda i, j: (i, 0)),
            pl.BlockSpec((tm, LANES), lambda i, j: (i, 0)),
            pl.BlockSpec((None, D, tn), lambda i, j: (layer, 0, j)),
            pl.BlockSpec((None, D, n_small), lambda i, j: (layer, 0, 0), pipeline_mode=pl.Buffered(1)),
            wo_in,
        ],
        out_specs=[
            pl.BlockSpec((tm, tn), lambda i, j: (i, j)),
            pl.BlockSpec((tm, n_small), lambda i, j: (i, 0)),
            wo_out,
        ],
        out_shape=[
            jax.ShapeDtypeStruct((T, n_main), BF16),
            jax.ShapeDtypeStruct((T, n_small), F32),
            wo_shape,
        ],
        compiler_params=_cparams(("arbitrary", "arbitrary")),
        name="inproj",
    )(xg, ssq, w_main, w_small, wo32)


def _pool_kernel(cur_ref, prev_ref, w_ref, s_ref, o_ref, ext_ref, lvl_ref, *, tm, blocks_per_seq):
    blk = pl.program_id(0) % blocks_per_seq
    cur = cur_ref[...].astype(F32)
    ext_ref[0:POOL_HALO, :] = jnp.where(blk == 0, 0.0, prev_ref[...].astype(F32))
    ext_ref[POOL_HALO:, :] = cur
    pos = blk * tm + lax.broadcasted_iota(jnp.int32, (tm, 1), 0)
    rows = tm + POOL_HALO
    for gi, win in enumerate(POOL_WINDOWS):
        cols = slice(gi * POOL_GROUP, (gi + 1) * POOL_GROUP)
        u = cur[:, cols]
        level = lambda a, b: ext_ref[a:b, cols]
        w, k = 1, 0
        while 2 * w < win:
            k += 1
            start = 8 * k
            lvl_ref[k % 2, start:rows, :] = level(start, rows) + level(start - w, rows - w)
            level = functools.partial(lambda buf, a, b: lvl_ref[buf, a:b, :], k % 2)
            w *= 2
        acc = level(POOL_HALO, rows) + level(POOL_HALO - w, rows - w)
        inv_count = 1.0 / jnp.minimum(pos + 1, win).astype(F32)
        p = (acc * inv_count - u).astype(BF16)
        y = _dot(p, w_ref[gi]) * s_ref[:, cols]
        o_ref[:, cols] = y.astype(BF16)


def _pool(z, w_pool, s_pool, layer, seq, tm=512):
    T = z.shape[0]
    halo_per_tile = tm // POOL_HALO
    return pl.pallas_call(
        functools.partial(_pool_kernel, tm=tm, blocks_per_seq=seq // tm),
        grid=(T // tm,),
        in_specs=[
            pl.BlockSpec((tm, POOL_WIDTH), lambda i: (i, Z_POOL // POOL_WIDTH)),
            pl.BlockSpec((POOL_HALO, POOL_WIDTH),
                         lambda i: (jnp.maximum(i * halo_per_tile - 1, 0), Z_POOL // POOL_WIDTH)),
            pl.BlockSpec((None, len(POOL_WINDOWS), POOL_GROUP, POOL_GROUP), lambda i: (layer, 0, 0, 0)),
            pl.BlockSpec((None, 1, POOL_WIDTH), lambda i: (layer, 0, 0)),
        ],
        out_specs=pl.BlockSpec((tm, POOL_WIDTH), lambda i: (i, 0)),
        out_shape=jax.ShapeDtypeStruct((T, POOL_WIDTH), BF16),
        scratch_shapes=[pltpu.VMEM((tm + POOL_HALO, POOL_WIDTH), F32),
                        pltpu.VMEM((2, tm + POOL_HALO, POOL_GROUP), F32)],
        compiler_params=_cparams(("parallel",)),
        name="pool",
    )(z, z, w_pool, s_pool)


def _fcum_kernel(zs_ref, bf_ref, qx_ref, kx_ref, carry_ref, *, tc, blocks_per_seq):
    @pl.when(pl.program_id(0) % blocks_per_seq == 0)
    def _():
        carry_ref[...] = jnp.zeros_like(carry_ref)

    lf = _log_sigmoid(zs_ref[...] + bf_ref[...])
    c = _cumsum_rows(_tril_ones(tc), lf) + carry_ref[...]
    carry_ref[...] = c[tc - 1:tc, :]

    lane = lax.broadcasted_iota(jnp.int32, (tc, LANES), 1)
    ones_q = ((lane >= 3) & (lane < 6)).astype(F32)
    for h in range(FOX_HEADS):
        big = jnp.broadcast_to(c[:, ZS_F_LANE + h:ZS_F_LANE + h + 1], (tc, LANES)) * (FOX_DH ** 0.5)
        hi = big.astype(BF16).astype(F32)
        rest = big - hi
        mid = rest.astype(BF16).astype(F32)
        lo = rest - mid
        trip = jnp.where(lane % 3 == 0, hi, jnp.where(lane % 3 == 1, mid, lo))
        qx_ref[:, h * LANES:(h + 1) * LANES] = jnp.where(lane < 3, trip, ones_q).astype(BF16)
        kx_ref[:, h * LANES:(h + 1) * LANES] = jnp.where(
            lane < 3, 1.0, jnp.where(lane < 6, -trip, 0.0)).astype(BF16)


def _fcum(zs, bf_pad, layer, seq, tc=512):
    T = zs.shape[0]
    x_spec = pl.BlockSpec((tc, FOX_HEADS * LANES), lambda i: (i, 0))
    x_shape = jax.ShapeDtypeStruct((T, FOX_HEADS * LANES), BF16)
    return pl.pallas_call(
        functools.partial(_fcum_kernel, tc=tc, blocks_per_seq=seq // tc),
        grid=(T // tc,),
        in_specs=[
            pl.BlockSpec((tc, LANES), lambda i: (i, ZS_SMALL // LANES)),
            pl.BlockSpec((None, 1, LANES), lambda i: (layer, 0, 0)),
        ],
        out_specs=[x_spec, x_spec],
        out_shape=[x_shape, x_shape],
        scratch_shapes=[pltpu.VMEM((1, LANES), F32)],
        compiler_params=_cparams(("arbitrary",)),
        name="fcum",
    )(zs, bf_pad)


def _gla_kernel(q_ref, k_ref, v_ref, r_ref, a_ref, wa_ref, ba_ref, gg_ref, o_ref,
                st_ref, g_ref, *, tg, hp):
    @pl.when(pl.program_id(2) == 0)
    def _():
        st_ref[...] = jnp.zeros_like(st_ref)

    logits = _dot(a_ref[...].astype(BF16), wa_ref[...]) + ba_ref[...]
    g_ref[...] = _log_sigmoid(logits) * (1.0 / GLA_GATE_NORMALIZER)

    C, SB = GLA_CHUNK, GLA_SUB
    nsb = C // SB
    row = lax.broadcasted_iota(jnp.int32, (C, C), 0)
    col = lax.broadcasted_iota(jnp.int32, (C, C), 1)
    tril = _tril_ones(C)
    sdiff = row // SB - col // SB
    lane_sb = lax.broadcasted_iota(jnp.int32, (SB, C), 1)
    row_sb = lax.broadcasted_iota(jnp.int32, (SB, C), 0)
    gain = gg_ref[...]

    def rows4(parts):
        return jnp.concatenate([jnp.broadcast_to(p, (SB, GLA_DK)) for p in parts], axis=0)

    def one_head(hh, r0):
        kcols = slice(hh * GLA_DK, (hh + 1) * GLA_DK)
        vcols = slice(hh * GLA_DV, (hh + 1) * GLA_DV)
        q = q_ref[pl.ds(r0, C), kcols].astype(F32) * (GLA_DK ** -0.5)
        k = k_ref[pl.ds(r0, C), kcols].astype(F32)
        v = v_ref[pl.ds(r0, C), vcols]
        g = g_ref[pl.ds(r0, C), kcols]
        b = _cumsum_rows(tril, g)
        edge = [jnp.zeros((1, GLA_DK), F32)] + [b[SB * (i + 1) - 1:SB * (i + 1), :] for i in range(nsb)]
        zero = edge[0]
        rb = rows4(edge[0:nsb])
        rn = rows4(edge[1:nsb + 1])
        qt = q * jnp.exp(b - rb)
        kt = (k * jnp.exp(rn - b)).astype(BF16)
        a_off = jnp.where(sdiff == 1, _dot_nt(qt.astype(BF16), kt), 0.0)
        for s in range(2, nsb):
            d = rows4([zero] * s + [jnp.exp(edge[i] - edge[i - s + 1]) for i in range(s, nsb)])
            a_off = a_off + jnp.where(sdiff == s, _dot_nt((qt * d).astype(BF16), kt), 0.0)
        a_rows = []
        b2 = b * float(np.log2(np.e))
        for sb in range(nsb):
            sl = slice(SB * sb, SB * (sb + 1))
            q_s, k_s, b_s = q[sl], k[sl], b2[sl]
            acc = a_off[sl]
            for j in range(SB):
                e = jnp.exp2(b_s - b_s[j:j + 1])
                colv = jnp.sum(q_s * (k_s[j:j + 1] * e), axis=-1, keepdims=True)
                acc = jnp.where((lane_sb == SB * sb + j) & (row_sb >= j), colv, acc)
            a_rows.append(acc)
        a = jnp.concatenate(a_rows, axis=0).astype(BF16)

        st = st_ref[hh]
        o = _dot(a, v) + _dot_nt((q * jnp.exp(b)).astype(BF16), st.astype(BF16))
        b_last = edge[nsb]
        k_dec = (k * jnp.exp(b_last - b)).astype(BF16)
        st_ref[hh] = st * jnp.exp(b_last) + _dot_tn(v, k_dec)

        o = o * lax.rsqrt(jnp.mean(o * o, axis=-1, keepdims=True) + EPS) * gain
        r = r_ref[pl.ds(r0, C), vcols].astype(F32)
        o_ref[pl.ds(r0, C), vcols] = (o * (r * _sigmoid(r))).astype(BF16)

    def chunk(c, carry):
        r0 = pl.multiple_of(c * C, C)
        for hh in range(hp):
            one_head(hh, r0)
        return carry

    lax.fori_loop(0, tg // C, chunk, 0, unroll=4)


def _gla(z, zs, wa_t, b_alpha, g_gla, layer, batch, seq, tg=512, hp=4):
    T = z.shape[0]
    nb = seq // tg
    kw, vw = hp * GLA_DK, hp * GLA_DV
    qb, kb, vb, rb = Z_GQ // kw, Z_GK // kw, Z_GV // vw, Z_GR // vw
    return pl.pallas_call(
        functools.partial(_gla_kernel, tg=tg, hp=hp),
        grid=(batch, GLA_HEADS // hp, nb),
        in_specs=[
            pl.BlockSpec((tg, kw), lambda b, h, n: (b * nb + n, qb + h)),
            pl.BlockSpec((tg, kw), lambda b, h, n: (b * nb + n, kb + h)),
            pl.BlockSpec((tg, vw), lambda b, h, n: (b * nb + n, vb + h)),
            pl.BlockSpec((tg, vw), lambda b, h, n: (b * nb + n, rb + h)),
            pl.BlockSpec((tg, LANES), lambda b, h, n: (b * nb + n, ZS_SMALL // LANES)),
            pl.BlockSpec((None, LANES, kw), lambda b, h, n: (layer, 0, h)),
            pl.BlockSpec((None, 1, kw), lambda b, h, n: (layer, 0, h)),
            pl.BlockSpec((None, 1, GLA_DV), lambda b, h, n: (layer, 0, 0)),
        ],
        out_specs=pl.BlockSpec((tg, vw), lambda b, h, n: (b * nb + n, h)),
        out_shape=jax.ShapeDtypeStruct((T, GLA_V_W), BF16),
        scratch_shapes=[pltpu.VMEM((hp, GLA_DV, GLA_DK), F32), pltpu.VMEM((tg, kw), F32)],
        compiler_params=_cparams(("parallel", "parallel", "arbitrary")),
        name="gla",
    )(z, z, z, z, zs, wa_t, b_alpha, g_gla)


def _fox_kernel(qi_ref, ki_ref, q_ref, qx_ref, k_ref, kx_ref, v_ref, *rest, tq, tk, strip, rider_slabs):
    n_riders = len(rider_slabs)
    riders_in, o_ref, riders_out = rest[:n_riders], rest[n_riders], rest[n_riders + 1:2 * n_riders + 1]
    m_ref, l_ref, acc_ref, qa_ref, s_ref = rest[2 * n_riders + 1:]
    t = pl.program_id(2)
    _ride_casts(riders_in, riders_out, rider_slabs,
                (pl.program_id(0) * pl.num_programs(1) + pl.program_id(1)) * pl.num_programs(2) + t)
    qi = qi_ref[t]
    ki = ki_ref[t]
    exp2_scale = (FOX_DH ** -0.5) * float(np.log2(np.e))

    @pl.when(ki == 0)
    def _():
        m_ref[...] = jnp.full_like(m_ref, -jnp.inf)
        l_ref[...] = jnp.zeros_like(l_ref)
        acc_ref[...] = jnp.zeros_like(acc_ref)
        qa_ref[:, 0:FOX_DH] = q_ref[...]
        qa_ref[:, FOX_DH:] = qx_ref[...]

    def step(diag):
        ka = jnp.concatenate([k_ref[...], kx_ref[...]], axis=1)
        v = v_ref[...]
        nstrip = tq // strip

        def width(c):
            return (c + 1) * strip if diag else tk

        def scores(c):
            s_ref[c % 2, :, :width(c)] = _dot_nt(qa_ref[c * strip:(c + 1) * strip, :], ka[:width(c)])

        scores(0)
        for c in range(nstrip):
            rows = slice(c * strip, (c + 1) * strip)
            ncol = width(c)
            if c + 1 < nstrip:
                scores(c + 1)
            s = s_ref[c % 2, :, :ncol]
            if diag:
                row = lax.broadcasted_iota(jnp.int32, (strip, ncol), 0) + c * strip
                col = lax.broadcasted_iota(jnp.int32, (strip, ncol), 1)
                s = jnp.where(col <= row, s, -jnp.inf)
            m_old = m_ref[rows, :]
            m_new = jnp.maximum(m_old, jnp.max(s, axis=-1, keepdims=True))
            p = jnp.exp2((s - m_new) * exp2_scale)
            alpha = jnp.exp2((m_old - m_new) * exp2_scale)
            l_ref[rows, :] = alpha * l_ref[rows, :] + jnp.sum(p, axis=-1, keepdims=True)
            acc_ref[rows, :] = alpha * acc_ref[rows, :] + _dot(p.astype(BF16), v[:ncol])
            m_ref[rows, :] = m_new

    @pl.when(ki < qi)
    def _():
        step(False)

    @pl.when(ki == qi)
    def _():
        step(True)
        o_ref[...] = (acc_ref[...] / l_ref[...]).astype(BF16)


def _fox(z, qx, kx, batch, seq, riders32, layer, tq=2048, strip=512):
    T = z.shape[0]
    tk = tq
    nq = seq // tq
    pairs = [(qi, ki) for qi in range(nq) for ki in range(qi + 1)]
    qi_tab = jnp.asarray(np.array([p[0] for p in pairs], np.int32))
    ki_tab = jnp.asarray(np.array([p[1] for p in pairs], np.int32))
    qb, kb, vb = Z_FQ // FOX_DH, Z_FK // FOX_DH, Z_FV // FOX_DH
    npairs = len(pairs)
    riders = [_cast_rider(w, layer, batch * FOX_HEADS * npairs,
                          lambda b, h, t, *_: (b * FOX_HEADS + h) * npairs + t) for w in riders32]
    grid_spec = pltpu.PrefetchScalarGridSpec(
        num_scalar_prefetch=2,
        grid=(batch, FOX_HEADS, len(pairs)),
        in_specs=[
            pl.BlockSpec((tq, FOX_DH), lambda b, h, t, qt, kt: (b * nq + qt[t], qb + h)),
            pl.BlockSpec((tq, LANES), lambda b, h, t, qt, kt: (b * nq + qt[t], h)),
            pl.BlockSpec((tk, FOX_DH), lambda b, h, t, qt, kt: (b * nq + kt[t], kb + h)),
            pl.BlockSpec((tk, LANES), lambda b, h, t, qt, kt: (b * nq + kt[t], h)),
            pl.BlockSpec((tk, FOX_DH), lambda b, h, t, qt, kt: (b * nq + kt[t], vb + h)),
        ] + [r[0] for r in riders],
        out_specs=[pl.BlockSpec((tq, FOX_DH), lambda b, h, t, qt, kt: (b * nq + qt[t], h))]
        + [r[1] for r in riders],
        scratch_shapes=[
            pltpu.VMEM((tq, 1), F32), pltpu.VMEM((tq, 1), F32),
            pltpu.VMEM((tq, FOX_DH), F32), pltpu.VMEM((tq, FOX_DH + LANES), BF16),
            pltpu.VMEM((2, strip, tk), F32),
        ],
    )
    out = pl.pallas_call(
        functools.partial(_fox_kernel, tq=tq, tk=tk, strip=strip, rider_slabs=tuple(r[3] for r in riders)),
        grid_spec=grid_spec,
        out_shape=[jax.ShapeDtypeStruct((T, FOX_W), BF16)] + [r[2] for r in riders],
        compiler_params=_cparams(("arbitrary", "arbitrary", "arbitrary")),
        name="fox",
    )(qi_tab, ki_tab, z, qx, z, kx, z, *riders32)
    return out[0], out[1:]


def _merge_kernel(yp_ref, yg_ref, yf_ref, gl_ref, wp_ref, wg_ref, wf_ref,
                  u0_ref, u1_ref, u2_ref, b0_ref, b1_ref, b2_ref, o_ref):
    gl = gl_ref[...].astype(BF16)
    out = _sigmoid(_dot(gl, u0_ref[...]) + b0_ref[...]) * _dot(yp_ref[...], wp_ref[...])
    out = out + _sigmoid(_dot(gl, u1_ref[...]) + b1_ref[...]) * _dot(yg_ref[...], wg_ref[...])
    out = out + _sigmoid(_dot(gl, u2_ref[...]) + b2_ref[...]) * _dot(yf_ref[...], wf_ref[...])
    o_ref[...] = out.astype(BF16)


def _merge(yp, yg, yf, zs, wpp, wpg, wpf, wgu, b_gate, layer, tm=1024, tn=1024):
    T = yp.shape[0]
    nj = D_MODEL // tn
    y_spec = pl.BlockSpec((tm, POOL_WIDTH), lambda i, j: (i, 0))
    w_spec = pl.BlockSpec((POOL_WIDTH, tn), lambda i, j: (0, j))

    def u_spec(br):
        return pl.BlockSpec((GATE_RANK, tn), lambda i, j: (0, br * nj + j))

    def b_spec(br):
        return pl.BlockSpec((None, 1, tn), lambda i, j: (layer, 0, br * nj + j))

    return pl.pallas_call(
        _merge_kernel,
        grid=(T // tm, nj),
        in_specs=[y_spec, y_spec, y_spec,
                  pl.BlockSpec((tm, GATE_RANK), lambda i, j: (i, ZS_GATE // GATE_RANK)),
                  w_spec, w_spec, w_spec,
                  u_spec(0), u_spec(1), u_spec(2), b_spec(0), b_spec(1), b_spec(2)],
        out_specs=pl.BlockSpec((tm, tn), lambda i, j: (i, j)),
        out_shape=jax.ShapeDtypeStruct((T, D_MODEL), BF16),
        compiler_params=_cparams(("parallel", "arbitrary")),
        name="merge",
    )(yp, yg, yf, zs, wpp, wpg, wpf, wgu, wgu, wgu, b_gate, b_gate, b_gate)


def _mm_res_kernel(a_ref, w_ref, res_ref, o_ref):
    o_ref[...] = res_ref[...] + _dot(a_ref[...], w_ref[...])


def _mm_res_norm_kernel(a_ref, w_ref, res_ref, g_ref, o_ref, xg_ref, ssq_ref):
    x = res_ref[...] + _dot(a_ref[...], w_ref[...])
    o_ref[...] = x
    xg_ref[...] = (x * g_ref[...]).astype(BF16)
    _accumulate_ssq(ssq_ref, jnp.sum(x * x, axis=-1, keepdims=True), pl.program_id(1))


def _mm_res(a, w, res, layer, tm, tn, name, gain=None, gain_layer=None):
    T = a.shape[0]
    K, N = w.shape[-2:]
    if w.ndim == 3:
        w_spec = pl.BlockSpec((None, K, tn), lambda i, j: (layer, 0, j))
    else:
        w_spec = pl.BlockSpec((K, tn), lambda i, j: (0, j))
    in_specs = [
        pl.BlockSpec((tm, K), lambda i, j: (i, 0)),
        w_spec,
        pl.BlockSpec((tm, tn), lambda i, j: (i, j)),
    ]
    x_spec = pl.BlockSpec((tm, tn), lambda i, j: (i, j))
    x_shape = jax.ShapeDtypeStruct((T, N), F32)
    if gain is None:
        return pl.pallas_call(
            _mm_res_kernel, grid=(T // tm, N // tn), in_specs=in_specs, out_specs=x_spec,
            out_shape=x_shape, compiler_params=_cparams(("parallel", "arbitrary")), name=name,
        )(a, w, res)
    return pl.pallas_call(
        _mm_res_norm_kernel,
        grid=(T // tm, N // tn),
        in_specs=in_specs + [pl.BlockSpec((None, 1, tn), lambda i, j: (gain_layer, 0, j))],
        out_specs=[x_spec, x_spec, pl.BlockSpec((tm, LANES), lambda i, j: (i, 0))],
        out_shape=[x_shape, jax.ShapeDtypeStruct((T, N), BF16), jax.ShapeDtypeStruct((T, LANES), F32)],
        compiler_params=_cparams(("parallel", "arbitrary")),
        name=name,
    )(a, w, res, gain)


def _swiglu_tile(xg, rinv, wg, wu):
    gate = _dot(xg, wg) * rinv
    up = _dot(xg, wu) * rinv
    return (gate * _sigmoid(gate) * up).astype(BF16)


def _ffn_up_head_kernel(xg_ref, ssq_ref, wg_ref, wu_ref, o_ref, wgb_ref, wub_ref):
    wg = wg_ref[...].astype(BF16)
    wu = wu_ref[...].astype(BF16)
    wgb_ref[...] = wg
    wub_ref[...] = wu
    o_ref[...] = _swiglu_tile(xg_ref[...], _row_rinv(ssq_ref), wg, wu)


def _ffn_up_head(xg, ssq, wg32, wu32, layer, n_out, tm=1024, tn=256):
    T, D = xg.shape
    N = wg32.shape[-1]
    w32_spec = pl.BlockSpec((None, D, tn), lambda j: (layer, 0, j))
    wb_spec = pl.BlockSpec((D, tn), lambda j: (0, j))
    wb_shape = jax.ShapeDtypeStruct((D, N), BF16)
    return pl.pallas_call(
        _ffn_up_head_kernel,
        grid=(N // tn,),
        in_specs=[
            pl.BlockSpec((tm, D), lambda j: (0, 0)),
            pl.BlockSpec((tm, LANES), lambda j: (0, 0)),
            w32_spec, w32_spec,
        ],
        out_specs=[pl.BlockSpec((tm, tn), lambda j: (0, j)), wb_spec, wb_spec],
        out_shape=[jax.ShapeDtypeStruct((T, n_out), BF16), wb_shape, wb_shape],
        compiler_params=_cparams(("parallel",)),
        name="ffn_up_head",
    )(xg, ssq, wg32, wu32)


def _ffn_up_kernel(xg_ref, ssq_ref, wg_ref, wu_ref, wd_ref, hid_in_ref, o_ref, wdb_ref, *, tail, wd_slabs):
    del hid_in_ref
    _ride_casts([wd_ref], [wdb_ref], [wd_slabs], pl.program_id(0) * pl.num_programs(1) + pl.program_id(1))
    rinv = _row_rinv(ssq_ref)
    last = pl.num_programs(1) - 1

    def tile(width):
        o_ref[:, :width] = _swiglu_tile(xg_ref[...], rinv, wg_ref[:, :width], wu_ref[:, :width])

    if tail == o_ref.shape[1]:
        tile(tail)
    else:
        pl.when(pl.program_id(1) != last)(lambda: tile(o_ref.shape[1]))
        pl.when(pl.program_id(1) == last)(lambda: tile(tail))


def _ffn_up(xg, ssq, wg32, wu32, wd32, layer, tm=1024, tn=512):
    T, D = xg.shape
    N = wg32.shape[-1]
    nj = pl.cdiv(N, tn)
    wd_in, wd_out, wd_shape, wd_slabs = _cast_rider(wd32, layer, (T // tm - 1) * nj, lambda i, j: i * nj + j)
    n_out = pl.cdiv(N, tn) * tn
    hid, wg, wu = _ffn_up_head(xg, ssq, wg32, wu32, layer, n_out, tm=tm)
    w_spec = pl.BlockSpec((D, tn), lambda i, j: (0, j))
    return pl.pallas_call(
        functools.partial(_ffn_up_kernel, tail=N - (pl.cdiv(N, tn) - 1) * tn, wd_slabs=wd_slabs),
        grid=(T // tm - 1, pl.cdiv(N, tn)),
        in_specs=[
            pl.BlockSpec((tm, D), lambda i, j: (i + 1, 0)),
            pl.BlockSpec((tm, LANES), lambda i, j: (i + 1, 0)),
            w_spec, w_spec,
            wd_in,
            pl.BlockSpec(memory_space=pl.ANY),
        ],
        out_specs=[pl.BlockSpec((tm, tn), lambda i, j: (i + 1, j)), wd_out],
        out_shape=[jax.ShapeDtypeStruct((T, n_out), BF16), wd_shape],
        input_output_aliases={5: 0},
        compiler_params=_cparams(("arbitrary", "arbitrary")),
        name="ffn_up",
    )(xg, ssq, wg, wu, wd32, hid)


def _norm_kernel(x_ref, g_ref, o_ref):
    o_ref[...] = _rms_rows(x_ref[...], g_ref[...])


def _final_norm(x, g, tm=512):
    T, D = x.shape
    return pl.pallas_call(
        _norm_kernel,
        grid=(T // tm,),
        in_specs=[pl.BlockSpec((tm, D), lambda i: (i, 0)), pl.BlockSpec((1, D), lambda i: (0, 0))],
        out_specs=pl.BlockSpec((tm, D), lambda i: (i, 0)),
        out_shape=jax.ShapeDtypeStruct((T, D), F32),
        compiler_params=_cparams(("parallel",)),
        name="final_norm",
    )(x, g)


def _split_offsets():
    sizes = (POOL_WIDTH, GLA_QK_W, GLA_QK_W, GLA_V_W, GLA_GATE_RANK, GLA_V_W,
             FOX_W, FOX_W, FOX_W, FOX_HEADS, GATE_RANK)
    offs = np.concatenate([[0], np.cumsum(sizes)])
    return [int(o) for o in offs]


def _prep_w_in(w_in):
    o = _split_offsets()
    a_lo, a_hi = o[4], o[5]
    f_lo, f_hi = o[9], o[10]
    gl_lo, gl_hi = o[10], o[11]
    wb = w_in.astype(BF16)
    main = jnp.concatenate([wb[..., :a_lo], wb[..., a_hi:f_lo]], axis=-1)
    pad = jnp.zeros(wb.shape[:-1] + (ZS_W - GATE_RANK - GLA_GATE_RANK - FOX_HEADS,), BF16)
    small = jnp.concatenate([wb[..., gl_lo:gl_hi], wb[..., a_lo:a_hi], wb[..., f_lo:f_hi], pad], axis=-1)
    return main, small


def kernel(x, g_mix, w_in, w_pool, s_pool, w_alpha, b_alpha, g_gla, b_f, w_gate_up, b_gate,
           w_proj_pool, w_proj_gla, w_proj_fox, w_o, g_ffn, w_ffn_gate, w_ffn_up, w_ffn_down, g_final):
    B, S, D = x.shape
    T = B * S
    L = g_mix.shape[0]
    xf = x.reshape(T, D)

    w_main, w_small = _prep_w_in(w_in)
    w_pool_b = w_pool.astype(BF16)
    wa = jnp.pad(w_alpha, ((0, 0), (ZS_A_LANE, LANES - ZS_A_LANE - GLA_GATE_RANK), (0, 0))).astype(BF16)
    bf_pad = jnp.pad(b_f, ((0, 0), (ZS_F_LANE, LANES - ZS_F_LANE - FOX_HEADS)))[:, None, :]
    g_mix3, g_ffn3 = g_mix[:, None, :], g_ffn[:, None, :]
    s_pool3, b_alpha3, g_gla3, b_gate3 = (a[:, None, :] for a in (s_pool, b_alpha, g_gla, b_gate))

    xg, ssq = _norm_prep(xf, g_mix3, 0)
    for l in range(L):
        z, zs, wo = _inproj(xg, ssq, w_main, w_small, w_o, l)
        y_pool = _pool(z, w_pool_b, s_pool3, l, S)
        qx, kx = _fcum(zs, bf_pad, l, S)
        y_gla = _gla(z, zs, wa, b_alpha3, g_gla3, l, B, S)
        y_fox, (wpp, wpg, wpf, wgu) = _fox(z, qx, kx, B, S, (w_proj_pool, w_proj_gla, w_proj_fox, w_gate_up), l)
        merged = _merge(y_pool, y_gla, y_fox, zs, wpp, wpg, wpf, wgu, b_gate3, l)
        xf, xg, ssq = _mm_res(merged, wo, xf, l, tm=1024, tn=512, name="out_proj", gain=g_ffn3, gain_layer=l)
        hid, wfd = _ffn_up(xg, ssq, w_ffn_gate, w_ffn_up, w_ffn_down, l)
        if l + 1 < L:
            xf, xg, ssq = _mm_res(hid, wfd, xf, l, tm=512, tn=512, name="ffn_down", gain=g_mix3, gain_layer=l + 1)
        else:
            xf = _mm_res(hid, wfd, xf, l, tm=512, tn=512, name="ffn_down")
    return _final_norm(xf, g_final[None, :]).reshape(B, S, D)
```

```python
import functools

import jax
import jax.numpy as jnp
import numpy as np
from jax import lax
from jax.experimental import pallas as pl
from jax.experimental.pallas import tpu as pltpu

F32 = jnp.float32
BF16 = jnp.bfloat16

D_MODEL = 4096
DEPTH = 4
EPS = 1e-6
POOL_WIDTH = 1024
POOL_WINDOWS = (2, 4, 8, 16)
POOL_GROUP = POOL_WIDTH // len(POOL_WINDOWS)
POOL_HALO = 32
GLA_HEADS = 4
GLA_DK = 128
GLA_DV = 256
GLA_QK_W = GLA_HEADS * GLA_DK
GLA_V_W = GLA_HEADS * GLA_DV
GLA_GATE_RANK = 16
GLA_GATE_NORMALIZER = 16.0
GLA_CHUNK = 64
GLA_SUB = 16
FOX_HEADS = 8
FOX_DH = 128
FOX_W = FOX_HEADS * FOX_DH
N_BRANCH = 3
GATE_RANK = 256
FFN_HIDDEN = ((8 * D_MODEL // 3 + 255) // 256) * 256

LANES = 128
Z_POOL, Z_GQ, Z_GK, Z_GV = 0, 1024, 1536, 2048
ZA_W = 3072
Z_GR, Z_FQ, Z_FK, Z_FV = 0, 1024, 2048, 3072
ZB_W = 4096
ZS_GATE = 0
ZS_SMALL = GATE_RANK
ZS_A_LANE = 0
ZS_F_LANE = GLA_GATE_RANK
ZS_W = GATE_RANK + LANES

VMEM_LIMIT = 56 * 1024 * 1024
CAST_SLAB_ROWS = 64


def _cparams(sem):
    return pltpu.CompilerParams(dimension_semantics=sem, vmem_limit_bytes=VMEM_LIMIT)


def _dot(a, b):
    return jnp.dot(a, b, preferred_element_type=F32)


def _dot_nt(a, b):
    return lax.dot_general(a, b, (((1,), (1,)), ((), ())), preferred_element_type=F32)


def _dot_tn(a, b):
    return lax.dot_general(a, b, (((0,), (0,)), ((), ())), preferred_element_type=F32)


def _log_sigmoid(x):
    return jnp.minimum(x, 0.0) - jnp.log(1.0 + jnp.exp(-jnp.abs(x)))


def _sigmoid(x):
    return 1.0 / (1.0 + jnp.exp(-x))


def _tril_ones(n):
    row = lax.broadcasted_iota(jnp.int32, (n, n), 0)
    col = lax.broadcasted_iota(jnp.int32, (n, n), 1)
    return (row >= col).astype(BF16)


def _cumsum_rows(tril, x):
    hi = x.astype(BF16)
    rest = x - hi.astype(F32)
    mid = rest.astype(BF16)
    lo = (rest - mid.astype(F32)).astype(BF16)
    return _dot(tril, hi) + _dot(tril, mid) + _dot(tril, lo)


def _rms_rows(x, g):
    return x * lax.rsqrt(jnp.mean(x * x, axis=-1, keepdims=True) + EPS) * g


def _cast_rider(w32, layer, steps, flat_step):
    R, C = w32.shape[-2:]
    units, rem = divmod(R, CAST_SLAB_ROWS)
    assert rem == 0
    nslab = max(d for d in range(1, units + 1) if units % d == 0 and d <= steps)
    rows = R // nslab

    def slab(*g):
        return jnp.minimum(flat_step(*g), nslab - 1)

    return (pl.BlockSpec((None, rows, C), lambda *g: (layer, slab(*g), 0)),
            pl.BlockSpec((rows, C), lambda *g: (slab(*g), 0)),
            jax.ShapeDtypeStruct((R, C), BF16), nslab)


def _ride_casts(srcs, dsts, nslabs, step):
    for src, dst, n in zip(srcs, dsts, nslabs):
        @pl.when(step < n)
        def _():
            dst[...] = src[...].astype(BF16)


def _row_rinv(ssq_ref):
    return lax.rsqrt(ssq_ref[:, 0:1] * (1.0 / D_MODEL) + EPS)


def _accumulate_ssq(ssq_ref, row_sums, j):
    part = jnp.broadcast_to(row_sums, ssq_ref.shape)

    @pl.when(j == 0)
    def _():
        ssq_ref[...] = part

    @pl.when(j != 0)
    def _():
        ssq_ref[...] += part


def _norm_prep_kernel(x_ref, g_ref, xg_ref, ssq_ref):
    x = x_ref[...]
    xg_ref[...] = (x * g_ref[...]).astype(BF16)
    ssq_ref[...] = jnp.broadcast_to(jnp.sum(x * x, axis=-1, keepdims=True), ssq_ref.shape)


def _norm_prep(x, g, layer, tm=512):
    T, D = x.shape
    return pl.pallas_call(
        _norm_prep_kernel,
        grid=(T // tm,),
        in_specs=[pl.BlockSpec((tm, D), lambda i: (i, 0)),
                  pl.BlockSpec((None, 1, D), lambda i: (layer, 0, 0))],
        out_specs=[pl.BlockSpec((tm, D), lambda i: (i, 0)),
                   pl.BlockSpec((tm, LANES), lambda i: (i, 0))],
        out_shape=[jax.ShapeDtypeStruct((T, D), BF16), jax.ShapeDtypeStruct((T, LANES), F32)],
        compiler_params=_cparams(("parallel",)),
        name="norm_prep",
    )(x, g)


def _inproj_kernel(xg_ref, ssq_ref, w_ref, ws_ref, z_ref, zs_ref):
    rinv = _row_rinv(ssq_ref)

    @pl.when(pl.program_id(1) == 0)
    def _():
        zs_ref[...] = _dot(xg_ref[...], ws_ref[...]) * rinv

    z_ref[...] = (_dot(xg_ref[...], w_ref[...]) * rinv).astype(BF16)


def _inproj(xg, ssq, w_main, w_small, layer, tm=1024, tn=1024):
    T, D = xg.shape
    n_main = w_main.shape[-1]
    n_small = w_small.shape[-1]
    return pl.pallas_call(
        _inproj_kernel,
        grid=(T // tm, n_main // tn),
        in_specs=[
            pl.BlockSpec((tm, D), lambda i, j: (i, 0)),
            pl.BlockSpec((tm, LANES), lambda i, j: (i, 0)),
            pl.BlockSpec((None, D, tn), lambda i, j: (layer, 0, j)),
            pl.BlockSpec((None, D, n_small), lambda i, j: (layer, 0, 0), pipeline_mode=pl.Buffered(1)),
        ],
        out_specs=[
            pl.BlockSpec((tm, tn), lambda i, j: (i, j)),
            pl.BlockSpec((tm, n_small), lambda i, j: (i, 0)),
        ],
        out_shape=[
            jax.ShapeDtypeStruct((T, n_main), BF16),
            jax.ShapeDtypeStruct((T, n_small), F32),
        ],
        compiler_params=_cparams(("parallel", "arbitrary")),
        name="inproj",
    )(xg, ssq, w_main, w_small)


def _inproj_b_kernel(xg_ref, ssq_ref, w_ref, wo_ref, z_ref, wob_ref, *, wo_slabs):
    _ride_casts([wo_ref], [wob_ref], [wo_slabs], pl.program_id(0) * pl.num_programs(1) + pl.program_id(1))
    z_ref[...] = (_dot(xg_ref[...], w_ref[...]) * _row_rinv(ssq_ref)).astype(BF16)


def _inproj_b(xg, ssq, w, wo32, layer, tm=1024, tn=1024):
    T, D = xg.shape
    N = w.shape[-1]
    nj = N // tn
    wo_in, wo_out, wo_shape, wo_slabs = _cast_rider(wo32, layer, (T // tm) * nj, lambda i, j: i * nj + j)
    return pl.pallas_call(
        functools.partial(_inproj_b_kernel, wo_slabs=wo_slabs),
        grid=(T // tm, nj),
        in_specs=[
            pl.BlockSpec((tm, D), lambda i, j: (i, 0)),
            pl.BlockSpec((tm, LANES), lambda i, j: (i, 0)),
            pl.BlockSpec((None, D, tn), lambda i, j: (layer, 0, j)),
            wo_in,
        ],
        out_specs=[pl.BlockSpec((tm, tn), lambda i, j: (i, j)), wo_out],
        out_shape=[jax.ShapeDtypeStruct((T, N), BF16), wo_shape],
        compiler_params=_cparams(("arbitrary", "arbitrary")),
        name="inproj_b",
    )(xg, ssq, w, wo32)


def _pool_kernel(cur_ref, prev_ref, w_ref, s_ref, o_ref, ext_ref, lvl_ref, *, tm, blocks_per_seq):
    blk = pl.program_id(0) % blocks_per_seq
    cur = cur_ref[...].astype(F32)
    ext_ref[0:POOL_HALO, :] = jnp.where(blk == 0, 0.0, prev_ref[...].astype(F32))
    ext_ref[POOL_HALO:, :] = cur
    pos = blk * tm + lax.broadcasted_iota(jnp.int32, (tm, 1), 0)
    rows = tm + POOL_HALO
    for gi, win in enumerate(POOL_WINDOWS):
        cols = slice(gi * POOL_GROUP, (gi + 1) * POOL_GROUP)
        u = cur[:, cols]
        level = lambda a, b: ext_ref[a:b, cols]
        w, k = 1, 0
        while 2 * w < win:
            k += 1
            start = 8 * k
            lvl_ref[k % 2, start:rows, :] = level(start, rows) + level(start - w, rows - w)
            level = functools.partial(lambda buf, a, b: lvl_ref[buf, a:b, :], k % 2)
            w *= 2
        acc = level(POOL_HALO, rows) + level(POOL_HALO - w, rows - w)
        inv_count = 1.0 / jnp.minimum(pos + 1, win).astype(F32)
        p = (acc * inv_count - u).astype(BF16)
        y = _dot(p, w_ref[gi]) * s_ref[:, cols]
        o_ref[:, cols] = y.astype(BF16)


def _pool(z, w_pool, s_pool, layer, seq, tm=512):
    T = z.shape[0]
    halo_per_tile = tm // POOL_HALO
    return pl.pallas_call(
        functools.partial(_pool_kernel, tm=tm, blocks_per_seq=seq // tm),
        grid=(T // tm,),
        in_specs=[
            pl.BlockSpec((tm, POOL_WIDTH), lambda i: (i, Z_POOL // POOL_WIDTH)),
            pl.BlockSpec((POOL_HALO, POOL_WIDTH),
                         lambda i: (jnp.maximum(i * halo_per_tile - 1, 0), Z_POOL // POOL_WIDTH)),
            pl.BlockSpec((None, len(POOL_WINDOWS), POOL_GROUP, POOL_GROUP), lambda i: (layer, 0, 0, 0)),
            pl.BlockSpec((None, 1, POOL_WIDTH), lambda i: (layer, 0, 0)),
        ],
        out_specs=pl.BlockSpec((tm, POOL_WIDTH), lambda i: (i, 0)),
        out_shape=jax.ShapeDtypeStruct((T, POOL_WIDTH), BF16),
        scratch_shapes=[pltpu.VMEM((tm + POOL_HALO, POOL_WIDTH), F32),
                        pltpu.VMEM((2, tm + POOL_HALO, POOL_GROUP), F32)],
        compiler_params=_cparams(("parallel",)),
        name="pool",
    )(z, z, w_pool, s_pool)


def _fcum_kernel(zs_ref, bf_ref, qx_ref, kx_ref, carry_ref, *, tc, blocks_per_seq):
    @pl.when(pl.program_id(0) % blocks_per_seq == 0)
    def _():
        carry_ref[...] = jnp.zeros_like(carry_ref)

    lf = _log_sigmoid(zs_ref[...] + bf_ref[...])
    c = _cumsum_rows(_tril_ones(tc), lf) + carry_ref[...]
    carry_ref[...] = c[tc - 1:tc, :]

    lane = lax.broadcasted_iota(jnp.int32, (tc, LANES), 1)
    ones_q = ((lane >= 3) & (lane < 6)).astype(F32)
    for h in range(FOX_HEADS):
        big = jnp.broadcast_to(c[:, ZS_F_LANE + h:ZS_F_LANE + h + 1], (tc, LANES)) * (FOX_DH ** 0.5)
        hi = big.astype(BF16).astype(F32)
        rest = big - hi
        mid = rest.astype(BF16).astype(F32)
        lo = rest - mid
        trip = jnp.where(lane % 3 == 0, hi, jnp.where(lane % 3 == 1, mid, lo))
        qx_ref[:, h * LANES:(h + 1) * LANES] = jnp.where(lane < 3, trip, ones_q).astype(BF16)
        kx_ref[:, h * LANES:(h + 1) * LANES] = jnp.where(
            lane < 3, 1.0, jnp.where(lane < 6, -trip, 0.0)).astype(BF16)


def _fcum(zs, bf_pad, layer, seq, tc=512):
    T = zs.shape[0]
    x_spec = pl.BlockSpec((tc, FOX_HEADS * LANES), lambda i: (i, 0))
    x_shape = jax.ShapeDtypeStruct((T, FOX_HEADS * LANES), BF16)
    return pl.pallas_call(
        functools.partial(_fcum_kernel, tc=tc, blocks_per_seq=seq // tc),
        grid=(T // tc,),
        in_specs=[
            pl.BlockSpec((tc, LANES), lambda i: (i, ZS_SMALL // LANES)),
            pl.BlockSpec((None, 1, LANES), lambda i: (layer, 0, 0)),
        ],
        out_specs=[x_spec, x_spec],
        out_shape=[x_shape, x_shape],
        scratch_shapes=[pltpu.VMEM((1, LANES), F32)],
        compiler_params=_cparams(("arbitrary",)),
        name="fcum",
    )(zs, bf_pad)


def _gla_kernel(q_ref, k_ref, v_ref, r_ref, a_ref, wa_ref, ba_ref, gg_ref, o_ref,
                st_ref, g_ref, *, tg, hp):
    @pl.when(pl.program_id(2) == 0)
    def _():
        st_ref[...] = jnp.zeros_like(st_ref)

    logits = _dot(a_ref[...].astype(BF16), wa_ref[...]) + ba_ref[...]
    g_ref[...] = _log_sigmoid(logits) * (1.0 / GLA_GATE_NORMALIZER)

    C, SB = GLA_CHUNK, GLA_SUB
    nsb = C // SB
    row = lax.broadcasted_iota(jnp.int32, (C, C), 0)
    col = lax.broadcasted_iota(jnp.int32, (C, C), 1)
    tril = _tril_ones(C)
    sdiff = row // SB - col // SB
    lane_sb = lax.broadcasted_iota(jnp.int32, (SB, C), 1)
    row_sb = lax.broadcasted_iota(jnp.int32, (SB, C), 0)
    gain = gg_ref[...]

    def rows4(parts):
        return jnp.concatenate([jnp.broadcast_to(p, (SB, GLA_DK)) for p in parts], axis=0)

    def one_head(hh, r0):
        kcols = slice(hh * GLA_DK, (hh + 1) * GLA_DK)
        vcols = slice(hh * GLA_DV, (hh + 1) * GLA_DV)
        q = q_ref[pl.ds(r0, C), kcols].astype(F32) * (GLA_DK ** -0.5)
        k = k_ref[pl.ds(r0, C), kcols].astype(F32)
        v = v_ref[pl.ds(r0, C), vcols]
        g = g_ref[pl.ds(r0, C), kcols]
        b = _cumsum_rows(tril, g)
        edge = [jnp.zeros((1, GLA_DK), F32)] + [b[SB * (i + 1) - 1:SB * (i + 1), :] for i in range(nsb)]
        zero = edge[0]
        rb = rows4(edge[0:nsb])
        rn = rows4(edge[1:nsb + 1])
        qt = q * jnp.exp(b - rb)
        kt = (k * jnp.exp(rn - b)).astype(BF16)
        a_off = jnp.where(sdiff == 1, _dot_nt(qt.astype(BF16), kt), 0.0)
        for s in range(2, nsb):
            d = rows4([zero] * s + [jnp.exp(edge[i] - edge[i - s + 1]) for i in range(s, nsb)])
            a_off = a_off + jnp.where(sdiff == s, _dot_nt((qt * d).astype(BF16), kt), 0.0)
        a_rows = []
        b2 = b * float(np.log2(np.e))
        for sb in range(nsb):
            sl = slice(SB * sb, SB * (sb + 1))
            q_s, k_s, b_s = q[sl], k[sl], b2[sl]
            acc = a_off[sl]
            for j in range(SB):
                e = jnp.exp2(b_s - b_s[j:j + 1])
                colv = jnp.sum(q_s * (k_s[j:j + 1] * e), axis=-1, keepdims=True)
                acc = jnp.where((lane_sb == SB * sb + j) & (row_sb >= j), colv, acc)
            a_rows.append(acc)
        a = jnp.concatenate(a_rows, axis=0).astype(BF16)

        st = st_ref[hh]
        o = _dot(a, v) + _dot_nt((q * jnp.exp(b)).astype(BF16), st.astype(BF16))
        b_last = edge[nsb]
        k_dec = (k * jnp.exp(b_last - b)).astype(BF16)
        st_ref[hh] = st * jnp.exp(b_last) + _dot_tn(v, k_dec)

        o = o * lax.rsqrt(jnp.mean(o * o, axis=-1, keepdims=True) + EPS) * gain
        r = r_ref[pl.ds(r0, C), vcols].astype(F32)
        o_ref[pl.ds(r0, C), vcols] = (o * (r * _sigmoid(r))).astype(BF16)

    def chunk(c, carry):
        r0 = pl.multiple_of(c * C, C)
        for hh in range(hp):
            one_head(hh, r0)
        return carry

    lax.fori_loop(0, tg // C, chunk, 0, unroll=4)


def _gla(z, zr, zs, wa_t, b_alpha, g_gla, layer, batch, seq, tg=512, hp=4):
    T = z.shape[0]
    nb = seq // tg
    kw, vw = hp * GLA_DK, hp * GLA_DV
    qb, kb, vb, rb = Z_GQ // kw, Z_GK // kw, Z_GV // vw, Z_GR // vw
    return pl.pallas_call(
        functools.partial(_gla_kernel, tg=tg, hp=hp),
        grid=(batch, GLA_HEADS // hp, nb),
        in_specs=[
            pl.BlockSpec((tg, kw), lambda b, h, n: (b * nb + n, qb + h)),
            pl.BlockSpec((tg, kw), lambda b, h, n: (b * nb + n, kb + h)),
            pl.BlockSpec((tg, vw), lambda b, h, n: (b * nb + n, vb + h)),
            pl.BlockSpec((tg, vw), lambda b, h, n: (b * nb + n, rb + h)),
            pl.BlockSpec((tg, LANES), lambda b, h, n: (b * nb + n, ZS_SMALL // LANES)),
            pl.BlockSpec((None, LANES, kw), lambda b, h, n: (layer, 0, h)),
            pl.BlockSpec((None, 1, kw), lambda b, h, n: (layer, 0, h)),
            pl.BlockSpec((None, 1, GLA_DV), lambda b, h, n: (layer, 0, 0)),
        ],
        out_specs=pl.BlockSpec((tg, vw), lambda b, h, n: (b * nb + n, h)),
        out_shape=jax.ShapeDtypeStruct((T, GLA_V_W), BF16),
        scratch_shapes=[pltpu.VMEM((hp, GLA_DV, GLA_DK), F32), pltpu.VMEM((tg, kw), F32)],
        compiler_params=_cparams(("parallel", "parallel", "arbitrary")),
        name="gla",
    )(z, z, z, zr, zs, wa_t, b_alpha, g_gla)


def _fox_kernel(qi_ref, ki_ref, q_ref, qx_ref, k_ref, kx_ref, v_ref, *rest, tq, tk, strip, rider_slabs):
    n_riders = len(rider_slabs)
    riders_in, o_ref, riders_out = rest[:n_riders], rest[n_riders], rest[n_riders + 1:2 * n_riders + 1]
    m_ref, l_ref, acc_ref, qa_ref, s_ref = rest[2 * n_riders + 1:]
    t = pl.program_id(2)
    _ride_casts(riders_in, riders_out, rider_slabs,
                (pl.program_id(0) * pl.num_programs(1) + pl.program_id(1)) * pl.num_programs(2) + t)
    qi = qi_ref[t]
    ki = ki_ref[t]
    exp2_scale = (FOX_DH ** -0.5) * float(np.log2(np.e))

    @pl.when(ki == 0)
    def _():
        m_ref[...] = jnp.full_like(m_ref, -jnp.inf)
        l_ref[...] = jnp.zeros_like(l_ref)
        acc_ref[...] = jnp.zeros_like(acc_ref)
        qa_ref[:, 0:FOX_DH] = q_ref[...]
        qa_ref[:, FOX_DH:] = qx_ref[...]

    def step(diag):
        ka = jnp.concatenate([k_ref[...], kx_ref[...]], axis=1)
        v = v_ref[...]
        nstrip = tq // strip

        def width(c):
            return (c + 1) * strip if diag else tk

        def scores(c):
            s_ref[c % 2, :, :width(c)] = _dot_nt(qa_ref[c * strip:(c + 1) * strip, :], ka[:width(c)])

        scores(0)
        for c in range(nstrip):
            rows = slice(c * strip, (c + 1) * strip)
            ncol = width(c)
            if c + 1 < nstrip:
                scores(c + 1)
            s = s_ref[c % 2, :, :ncol]
            if diag:
                row = lax.broadcasted_iota(jnp.int32, (strip, ncol), 0) + c * strip
                col = lax.broadcasted_iota(jnp.int32, (strip, ncol), 1)
                s = jnp.where(col <= row, s, -jnp.inf)
            m_old = m_ref[rows, :]
            m_new = jnp.maximum(m_old, jnp.max(s, axis=-1, keepdims=True))
            p = jnp.exp2((s - m_new) * exp2_scale)
            alpha = jnp.exp2((m_old - m_new) * exp2_scale)
            l_ref[rows, :] = alpha * l_ref[rows, :] + jnp.sum(p, axis=-1, keepdims=True)
            acc_ref[rows, :] = alpha * acc_ref[rows, :] + _dot(p.astype(BF16), v[:ncol])
            m_ref[rows, :] = m_new

    @pl.when(ki < qi)
    def _():
        step(False)

    @pl.when(ki == qi)
    def _():
        step(True)
        o_ref[...] = (acc_ref[...] / l_ref[...]).astype(BF16)


def _fox(z, qx, kx, batch, seq, riders32, layer, tq=2048, strip=512):
    T = z.shape[0]
    tk = tq
    nq = seq // tq
    pairs = [(qi, ki) for qi in range(nq) for ki in range(qi + 1)]
    qi_tab = jnp.asarray(np.array([p[0] for p in pairs], np.int32))
    ki_tab = jnp.asarray(np.array([p[1] for p in pairs], np.int32))
    qb, kb, vb = Z_FQ // FOX_DH, Z_FK // FOX_DH, Z_FV // FOX_DH
    npairs = len(pairs)
    riders = [_cast_rider(w, layer, batch * FOX_HEADS * npairs,
                          lambda b, h, t, *_: (b * FOX_HEADS + h) * npairs + t) for w in riders32]
    grid_spec = pltpu.PrefetchScalarGridSpec(
        num_scalar_prefetch=2,
        grid=(batch, FOX_HEADS, len(pairs)),
        in_specs=[
            pl.BlockSpec((tq, FOX_DH), lambda b, h, t, qt, kt: (b * nq + qt[t], qb + h)),
            pl.BlockSpec((tq, LANES), lambda b, h, t, qt, kt: (b * nq + qt[t], h)),
            pl.BlockSpec((tk, FOX_DH), lambda b, h, t, qt, kt: (b * nq + kt[t], kb + h)),
            pl.BlockSpec((tk, LANES), lambda b, h, t, qt, kt: (b * nq + kt[t], h)),
            pl.BlockSpec((tk, FOX_DH), lambda b, h, t, qt, kt: (b * nq + kt[t], vb + h)),
        ] + [r[0] for r in riders],
        out_specs=[pl.BlockSpec((tq, FOX_DH), lambda b, h, t, qt, kt: (b * nq + qt[t], h))]
        + [r[1] for r in riders],
        scratch_shapes=[
            pltpu.VMEM((tq, 1), F32), pltpu.VMEM((tq, 1), F32),
            pltpu.VMEM((tq, FOX_DH), F32), pltpu.VMEM((tq, FOX_DH + LANES), BF16),
            pltpu.VMEM((2, strip, tk), F32),
        ],
    )
    out = pl.pallas_call(
        functools.partial(_fox_kernel, tq=tq, tk=tk, strip=strip, rider_slabs=tuple(r[3] for r in riders)),
        grid_spec=grid_spec,
        out_shape=[jax.ShapeDtypeStruct((T, FOX_W), BF16)] + [r[2] for r in riders],
        compiler_params=_cparams(("arbitrary", "arbitrary", "arbitrary")),
        name="fox",
    )(qi_tab, ki_tab, z, qx, z, kx, z, *riders32)
    return out[0], out[1:]


def _merge_kernel(yp_ref, yg_ref, yf_ref, gl_ref, wp_ref, wg_ref, wf_ref,
                  u0_ref, u1_ref, u2_ref, b0_ref, b1_ref, b2_ref, o_ref):
    gl = gl_ref[...].astype(BF16)
    out = _sigmoid(_dot(gl, u0_ref[...]) + b0_ref[...]) * _dot(yp_ref[...], wp_ref[...])
    out = out + _sigmoid(_dot(gl, u1_ref[...]) + b1_ref[...]) * _dot(yg_ref[...], wg_ref[...])
    out = out + _sigmoid(_dot(gl, u2_ref[...]) + b2_ref[...]) * _dot(yf_ref[...], wf_ref[...])
    o_ref[...] = out.astype(BF16)


def _merge(yp, yg, yf, zs, wpp, wpg, wpf, wgu, b_gate, layer, tm=1024, tn=1024):
    T = yp.shape[0]
    nj = D_MODEL // tn
    y_spec = pl.BlockSpec((tm, POOL_WIDTH), lambda i, j: (i, 0))
    w_spec = pl.BlockSpec((POOL_WIDTH, tn), lambda i, j: (0, j))

    def u_spec(br):
        return pl.BlockSpec((GATE_RANK, tn), lambda i, j: (0, br * nj + j))

    def b_spec(br):
        return pl.BlockSpec((None, 1, tn), lambda i, j: (layer, 0, br * nj + j))

    return pl.pallas_call(
        _merge_kernel,
        grid=(T // tm, nj),
        in_specs=[y_spec, y_spec, y_spec,
                  pl.BlockSpec((tm, GATE_RANK), lambda i, j: (i, ZS_GATE // GATE_RANK)),
                  w_spec, w_spec, w_spec,
                  u_spec(0), u_spec(1), u_spec(2), b_spec(0), b_spec(1), b_spec(2)],
        out_specs=pl.BlockSpec((tm, tn), lambda i, j: (i, j)),
        out_shape=jax.ShapeDtypeStruct((T, D_MODEL), BF16),
        compiler_params=_cparams(("parallel", "arbitrary")),
        name="merge",
    )(yp, yg, yf, zs, wpp, wpg, wpf, wgu, wgu, wgu, b_gate, b_gate, b_gate)


def _mm_res_kernel(a_ref, w_ref, res_ref, o_ref):
    o_ref[...] = res_ref[...] + _dot(a_ref[...], w_ref[...])


def _mm_res_norm_kernel(a_ref, w_ref, res_ref, g_ref, o_ref, xg_ref, ssq_ref):
    x = res_ref[...] + _dot(a_ref[...], w_ref[...])
    o_ref[...] = x
    xg_ref[...] = (x * g_ref[...]).astype(BF16)
    _accumulate_ssq(ssq_ref, jnp.sum(x * x, axis=-1, keepdims=True), pl.program_id(1))


def _mm_res(a, w, res, layer, tm, tn, name, gain=None, gain_layer=None):
    T = a.shape[0]
    K, N = w.shape[-2:]
    if w.ndim == 3:
        w_spec = pl.BlockSpec((None, K, tn), lambda i, j: (layer, 0, j))
    else:
        w_spec = pl.BlockSpec((K, tn), lambda i, j: (0, j))
    in_specs = [
        pl.BlockSpec((tm, K), lambda i, j: (i, 0), pipeline_mode=pl.Buffered(1 if tn >= 1024 else 2)),
        w_spec,
        pl.BlockSpec((tm, tn), lambda i, j: (i, j)),
    ]
    x_spec = pl.BlockSpec((tm, tn), lambda i, j: (i, j))
    x_shape = jax.ShapeDtypeStruct((T, N), F32)
    if gain is None:
        return pl.pallas_call(
            _mm_res_kernel, grid=(T // tm, N // tn), in_specs=in_specs, out_specs=x_spec,
            out_shape=x_shape, compiler_params=_cparams(("parallel", "arbitrary")), name=name,
        )(a, w, res)
    return pl.pallas_call(
        _mm_res_norm_kernel,
        grid=(T // tm, N // tn),
        in_specs=in_specs + [pl.BlockSpec((None, 1, tn), lambda i, j: (gain_layer, 0, j))],
        out_specs=[x_spec, x_spec, pl.BlockSpec((tm, LANES), lambda i, j: (i, 0))],
        out_shape=[x_shape, jax.ShapeDtypeStruct((T, N), BF16), jax.ShapeDtypeStruct((T, LANES), F32)],
        compiler_params=_cparams(("parallel", "arbitrary")),
        name=name,
    )(a, w, res, gain)


def _swiglu_tile(xg, rinv, wg, wu):
    gate = _dot(xg, wg) * rinv
    up = _dot(xg, wu) * rinv
    return (gate * _sigmoid(gate) * up).astype(BF16)


def _ffn_up_head_kernel(xg_ref, ssq_ref, wg_ref, wu_ref, o_ref, wgb_ref, wub_ref):
    wg = wg_ref[...].astype(BF16)
    wu = wu_ref[...].astype(BF16)
    wgb_ref[...] = wg
    wub_ref[...] = wu
    o_ref[...] = _swiglu_tile(xg_ref[...], _row_rinv(ssq_ref), wg, wu)


def _ffn_up_head(xg, ssq, wg32, wu32, layer, n_out, tm=1024, tn=256):
    T, D = xg.shape
    N = wg32.shape[-1]
    w32_spec = pl.BlockSpec((None, D, tn), lambda j: (layer, 0, j))
    wb_spec = pl.BlockSpec((D, tn), lambda j: (0, j))
    wb_shape = jax.ShapeDtypeStruct((D, N), BF16)
    return pl.pallas_call(
        _ffn_up_head_kernel,
        grid=(N // tn,),
        in_specs=[
            pl.BlockSpec((tm, D), lambda j: (0, 0)),
            pl.BlockSpec((tm, LANES), lambda j: (0, 0)),
            w32_spec, w32_spec,
        ],
        out_specs=[pl.BlockSpec((tm, tn), lambda j: (0, j)), wb_spec, wb_spec],
        out_shape=[jax.ShapeDtypeStruct((T, n_out), BF16), wb_shape, wb_shape],
        compiler_params=_cparams(("parallel",)),
        name="ffn_up_head",
    )(xg, ssq, wg32, wu32)


def _ffn_up_kernel(xg_ref, ssq_ref, wg_ref, wu_ref, wd_ref, hid_in_ref, o_ref, wdb_ref, *, tail, wd_slabs):
    del hid_in_ref
    _ride_casts([wd_ref], [wdb_ref], [wd_slabs], pl.program_id(0) * pl.num_programs(1) + pl.program_id(1))
    rinv = _row_rinv(ssq_ref)
    last = pl.num_programs(1) - 1

    def tile(width):
        o_ref[:, :width] = _swiglu_tile(xg_ref[...], rinv, wg_ref[:, :width], wu_ref[:, :width])

    if tail == o_ref.shape[1]:
        tile(tail)
    else:
        pl.when(pl.program_id(1) != last)(lambda: tile(o_ref.shape[1]))
        pl.when(pl.program_id(1) == last)(lambda: tile(tail))


def _ffn_up(xg, ssq, wg32, wu32, wd32, layer, tm=1024, tn=512):
    T, D = xg.shape
    N = wg32.shape[-1]
    nj = pl.cdiv(N, tn)
    wd_in, wd_out, wd_shape, wd_slabs = _cast_rider(wd32, layer, (T // tm - 1) * nj, lambda i, j: i * nj + j)
    n_out = pl.cdiv(N, tn) * tn
    hid, wg, wu = _ffn_up_head(xg, ssq, wg32, wu32, layer, n_out, tm=tm)
    w_spec = pl.BlockSpec((D, tn), lambda i, j: (0, j))
    return pl.pallas_call(
        functools.partial(_ffn_up_kernel, tail=N - (pl.cdiv(N, tn) - 1) * tn, wd_slabs=wd_slabs),
        grid=(T // tm - 1, pl.cdiv(N, tn)),
        in_specs=[
            pl.BlockSpec((tm, D), lambda i, j: (i + 1, 0)),
            pl.BlockSpec((tm, LANES), lambda i, j: (i + 1, 0)),
            w_spec, w_spec,
            wd_in,
            pl.BlockSpec(memory_space=pl.ANY),
        ],
        out_specs=[pl.BlockSpec((tm, tn), lambda i, j: (i + 1, j)), wd_out],
        out_shape=[jax.ShapeDtypeStruct((T, n_out), BF16), wd_shape],
        input_output_aliases={5: 0},
        compiler_params=_cparams(("arbitrary", "arbitrary")),
        name="ffn_up",
    )(xg, ssq, wg, wu, wd32, hid)


def _norm_kernel(x_ref, g_ref, o_ref):
    o_ref[...] = _rms_rows(x_ref[...], g_ref[...])


def _final_norm(x, g, tm=512):
    T, D = x.shape
    return pl.pallas_call(
        _norm_kernel,
        grid=(T // tm,),
        in_specs=[pl.BlockSpec((tm, D), lambda i: (i, 0)), pl.BlockSpec((1, D), lambda i: (0, 0))],
        out_specs=pl.BlockSpec((tm, D), lambda i: (i, 0)),
        out_shape=jax.ShapeDtypeStruct((T, D), F32),
        compiler_params=_cparams(("parallel",)),
        name="final_norm",
    )(x, g)


def _split_offsets():
    sizes = (POOL_WIDTH, GLA_QK_W, GLA_QK_W, GLA_V_W, GLA_GATE_RANK, GLA_V_W,
             FOX_W, FOX_W, FOX_W, FOX_HEADS, GATE_RANK)
    offs = np.concatenate([[0], np.cumsum(sizes)])
    return [int(o) for o in offs]


def _prep_w_in(w_in):
    o = _split_offsets()
    a_lo, a_hi = o[4], o[5]
    f_lo, f_hi = o[9], o[10]
    gl_lo, gl_hi = o[10], o[11]
    assert a_lo == ZA_W and f_lo - a_hi == ZB_W
    wb = w_in.astype(BF16)
    pad = jnp.zeros(wb.shape[:-1] + (ZS_W - GATE_RANK - GLA_GATE_RANK - FOX_HEADS,), BF16)
    small = jnp.concatenate([wb[..., gl_lo:gl_hi], wb[..., a_lo:a_hi], wb[..., f_lo:f_hi], pad], axis=-1)
    return wb[..., :a_lo], wb[..., a_hi:f_lo], small


def kernel(x, g_mix, w_in, w_pool, s_pool, w_alpha, b_alpha, g_gla, b_f, w_gate_up, b_gate,
           w_proj_pool, w_proj_gla, w_proj_fox, w_o, g_ffn, w_ffn_gate, w_ffn_up, w_ffn_down, g_final):
    B, S, D = x.shape
    T = B * S
    L = g_mix.shape[0]
    xf = x.reshape(T, D)

    w_a, w_b, w_small = _prep_w_in(w_in)
    w_pool_b = w_pool.astype(BF16)
    wa = jnp.pad(w_alpha, ((0, 0), (ZS_A_LANE, LANES - ZS_A_LANE - GLA_GATE_RANK), (0, 0))).astype(BF16)
    bf_pad = jnp.pad(b_f, ((0, 0), (ZS_F_LANE, LANES - ZS_F_LANE - FOX_HEADS)))[:, None, :]
    g_mix3, g_ffn3 = g_mix[:, None, :], g_ffn[:, None, :]
    s_pool3, b_alpha3, g_gla3, b_gate3 = (a[:, None, :] for a in (s_pool, b_alpha, g_gla, b_gate))

    xg, ssq = _norm_prep(xf, g_mix3, 0)
    for l in range(L):
        za, zs = _inproj(xg, ssq, w_a, w_small, l)
        zb, wo = _inproj_b(xg, ssq, w_b, w_o, l)
        y_pool = _pool(za, w_pool_b, s_pool3, l, S)
        qx, kx = _fcum(zs, bf_pad, l, S)
        y_gla = _gla(za, zb, zs, wa, b_alpha3, g_gla3, l, B, S)
        y_fox, (wpp, wpg, wpf, wgu) = _fox(zb, qx, kx, B, S, (w_proj_pool, w_proj_gla, w_proj_fox, w_gate_up), l)
        merged = _merge(y_pool, y_gla, y_fox, zs, wpp, wpg, wpf, wgu, b_gate3, l)
        xf, xg, ssq = _mm_res(merged, wo, xf, l, tm=1024, tn=1024, name="out_proj", gain=g_ffn3, gain_layer=l)
        hid, wfd = _ffn_up(xg, ssq, w_ffn_gate, w_ffn_up, w_ffn_down, l)
        if l + 1 < L:
            xf, xg, ssq = _mm_res(hid, wfd, xf, l, tm=512, tn=512, name="ffn_down", gain=g_mix3, gain_layer=l + 1)
        else:
            xf = _mm_res(hid, wfd, xf, l, tm=512, tn=512, name="ffn_down")
    return _final_norm(xf, g_final[None, :]).reshape(B, S, D)
```
